```python
import jax, jax.numpy as jnp
from jax import lax
import numpy as np

D_MODEL = 2048
BATCH = 4
SEQ = 2048
DEPTH = 4
DEC_BATCH = 2
DEC_SEQ = 8192
PAST_LEN = 128

FOURIER_WIDTH = D_MODEL // 4
FOURIER_GROUPS = 4
FOURIER_GROUP_DIM = FOURIER_WIDTH // FOURIER_GROUPS
HG_WIDTH = D_MODEL - FOURIER_WIDTH
HG_HEAD_DIM = 128
HG_HEADS = HG_WIDTH // HG_HEAD_DIM
IN_WIDTH = FOURIER_WIDTH + 5 * HG_WIDTH + 2 * D_MODEL
D_FF = 128 * ((8 * D_MODEL // 3 + 127) // 128)
CHUNK = 64
NORM_EPS = 1e-6
MIN_FORGET = 1e-30

kernel_name = "fnet_hgrn2_macaron_encoder"


def _rmsnorm(x, gain):
    xf = x.astype(jnp.float32)
    y = xf * lax.rsqrt(jnp.mean(xf * xf, axis=-1, keepdims=True) + NORM_EPS)
    return (y * gain.astype(jnp.float32)).astype(x.dtype)


def _swiglu_half(x, gain, w_in, w_down):
    u = _rmsnorm(x, gain)
    gate, up = jnp.split(u @ w_in, 2, axis=-1)
    return x + 0.5 * ((jax.nn.silu(gate) * up) @ w_down)


def _fourier_mix(u_f):
    b, s, _ = u_f.shape
    g = u_f.astype(jnp.float32).reshape(b, s, FOURIER_GROUPS, FOURIER_GROUP_DIM)
    y = jnp.fft.fft2(g, axes=(1, 3), norm="ortho").real
    return y.reshape(b, s, FOURIER_WIDTH).astype(u_f.dtype)


def _lower_bounds(lb_logits):
    p = jax.nn.softmax(lb_logits.astype(jnp.float32), axis=1)
    return jnp.maximum(jnp.cumsum(p, axis=1) - p[:, :1], 0.0)


def _forget(f_logit, lb):
    f = lb + (1.0 - lb) * jax.nn.sigmoid(f_logit)
    log_f = jnp.log(jnp.maximum(f, MIN_FORGET))
    k = (1.0 - lb) * jax.nn.sigmoid(-f_logit)
    return log_f, k


def _to_chunks(t):
    b, s, h, d = t.shape
    return t.reshape(b, s // CHUNK, CHUNK, h, d).transpose(1, 0, 3, 2, 4)


def _from_chunks(t):
    n, b, h, c, d = t.shape
    return t.transpose(1, 0, 3, 2, 4).reshape(b, n * c, h, d)


def _hgrn2_scan(q, k, v, log_f):
    qc, kc, vc = _to_chunks(q), _to_chunks(k), _to_chunks(v)
    bc = jnp.cumsum(_to_chunks(log_f), axis=-2)
    lower = jnp.tril(jnp.ones((CHUNK, CHUNK), dtype=bool))[:, :, None]

    def step(state, inp):
        q_c, k_c, v_c, b_c = inp
        rel = b_c[:, :, :, None, :] - b_c[:, :, None, :, :]
        decay = jnp.where(lower, jnp.exp(jnp.where(lower, rel, 0.0)), 0.0)
        scores = jnp.einsum("bhtk,bhsk,bhtsk->bhts", q_c, k_c, decay)
        o = (jnp.einsum("bhts,bhsv->bhtv", scores, v_c)
             + jnp.einsum("bhtk,bhkv->bhtv", q_c * jnp.exp(b_c), state))
        b_last = b_c[:, :, -1:, :]
        state = (jnp.exp(b_last)[:, :, 0, :, None] * state
                 + jnp.einsum("bhsk,bhsv->bhkv", k_c * jnp.exp(b_last - b_c), v_c))
        return state, o

    bsz, _, h, dk = q.shape
    init = jnp.zeros((bsz, h, dk, v.shape[-1]), jnp.float32)
    _, o = lax.scan(step, init, (qc, kc, vc, bc))
    return _from_chunks(o)


def _mixer(x, gain, w_in, lb_fwd, lb_bwd, hg_gain, w_fourier_out, w_hgrn_out, w_out):
    bsz, s, _ = x.shape
    u = _rmsnorm(x, gain)
    proj = u @ w_in
    idx = [FOURIER_WIDTH + j * HG_WIDTH for j in range(6)] + [FOURIER_WIDTH + 5 * HG_WIDTH + D_MODEL]
    u_f, q, z_fwd, z_bwd, i_in, g_out, gate_a, gate_b = jnp.split(proj, idx, axis=-1)

    branch_a = _fourier_mix(u_f) @ w_fourier_out

    def heads(t):
        return t.astype(jnp.float32).reshape(bsz, s, HG_HEADS, HG_HEAD_DIM)

    qh = jax.nn.silu(heads(q))
    vh = heads(i_in)
    logf_f, k_f = _forget(heads(z_fwd), lb_fwd.reshape(HG_HEADS, HG_HEAD_DIM))
    logf_b, k_b = _forget(heads(z_bwd), lb_bwd.reshape(HG_HEADS, HG_HEAD_DIM))
    o_fwd = _hgrn2_scan(qh, k_f, vh, logf_f)
    o_bwd = jnp.flip(_hgrn2_scan(jnp.flip(qh, 1), jnp.flip(k_b, 1), jnp.flip(vh, 1),
                                 jnp.flip(logf_b, 1)), 1)
    o = o_fwd + o_bwd
    o = o * lax.rsqrt(jnp.mean(o * o, axis=-1, keepdims=True) + NORM_EPS)
    o = o.reshape(bsz, s, HG_WIDTH) * hg_gain.astype(jnp.float32) * jax.nn.silu(g_out.astype(jnp.float32))
    branch_b = o.astype(x.dtype) @ w_hgrn_out

    merged = jax.nn.sigmoid(gate_a) * branch_a + jax.nn.sigmoid(gate_b) * branch_b
    return x + merged @ w_out


def _trunk(x, ffn1_norm, ffn1_w_in, ffn1_w_down, mix_norm, w_in, lb_logits, hg_norm,
           w_fourier_out, w_hgrn_out, w_out, ffn2_norm, ffn2_w_in, ffn2_w_down, final_norm):
    lb = _lower_bounds(lb_logits)
    for l in range(DEPTH):
        x = _swiglu_half(x, ffn1_norm[l], ffn1_w_in[l], ffn1_w_down[l])
        x = _mixer(x, mix_norm[l], w_in[l], lb[0, l], lb[1, l], hg_norm[l],
                   w_fourier_out[l], w_hgrn_out[l], w_out[l])
        x = _swiglu_half(x, ffn2_norm[l], ffn2_w_in[l], ffn2_w_down[l])
    return _rmsnorm(x, final_norm)


def setup_inputs(seed: int = 0) -> dict:
    key = jax.random.key(seed)
    ks = jax.random.split(key, 16)
    f32 = jnp.float32

    def w(k, shape, fan_in):
        return jax.random.normal(k, shape, f32) * (fan_in ** -0.5)

    def g(k, shape):
        return 1.0 + 0.02 * jax.random.normal(k, shape, f32)

    return {
        "x_prompt": jax.random.normal(ks[0], (BATCH, SEQ, D_MODEL), f32),
        "x_sample": jax.random.normal(ks[1], (DEC_BATCH, DEC_SEQ, D_MODEL), f32),
        "ffn1_norm": g(ks[2], (DEPTH, D_MODEL)),
        "ffn1_w_in": w(ks[3], (DEPTH, D_MODEL, 2 * D_FF), D_MODEL),
        "ffn1_w_down": w(ks[4], (DEPTH, D_FF, D_MODEL), D_FF),
        "mix_norm": g(ks[5], (DEPTH, D_MODEL)),
        "w_in": w(ks[6], (DEPTH, D_MODEL, IN_WIDTH), D_MODEL),
        "lb_logits": 0.5 * jax.random.normal(ks[7], (2, DEPTH, HG_WIDTH), f32),
        "hg_norm": g(ks[8], (DEPTH, HG_WIDTH)),
        "w_fourier_out": w(ks[9], (DEPTH, FOURIER_WIDTH, D_MODEL), FOURIER_WIDTH),
        "w_hgrn_out": w(ks[10], (DEPTH, HG_WIDTH, D_MODEL), HG_WIDTH),
        "w_out": w(ks[11], (DEPTH, D_MODEL, D_MODEL), D_MODEL),
        "ffn2_norm": g(ks[12], (DEPTH, D_MODEL)),
        "ffn2_w_in": w(ks[13], (DEPTH, D_MODEL, 2 * D_FF), D_MODEL),
        "ffn2_w_down": w(ks[14], (DEPTH, D_FF, D_MODEL), D_FF),
        "final_norm": g(ks[15], (D_MODEL,)),
    }


def reference(x_prompt, x_sample, ffn1_norm, ffn1_w_in, ffn1_w_down, mix_norm, w_in, lb_logits,
              hg_norm, w_fourier_out, w_hgrn_out, w_out, ffn2_norm, ffn2_w_in, ffn2_w_down,
              final_norm):
    y_prompt = _trunk(x_prompt, ffn1_norm, ffn1_w_in, ffn1_w_down, mix_norm, w_in, lb_logits,
                      hg_norm, w_fourier_out, w_hgrn_out, w_out, ffn2_norm, ffn2_w_in,
                      ffn2_w_down, final_norm)
    y_sample = _trunk(x_sample, ffn1_norm, ffn1_w_in, ffn1_w_down, mix_norm, w_in, lb_logits,
                      hg_norm, w_fourier_out, w_hgrn_out, w_out, ffn2_norm, ffn2_w_in,
                      ffn2_w_down, final_norm)
    return (y_prompt, y_sample)
```

```python
import functools

import ml_dtypes
import numpy as np
import jax
import jax.numpy as jnp
from jax import lax
from jax.experimental import pallas as pl
from jax.experimental.pallas import tpu as pltpu

F32 = jnp.float32
BF16 = jnp.bfloat16

NORM_EPS = 1e-6
MIN_FORGET = 1e-30
LANES = 128
SUBLANES = 8
HEAD_DIM = 128
FOURIER_GROUP_DIM = 128
SCAN_CHUNK = 128
DFT_INNER = 128
VMEM_LIMIT = 56 * 1024 * 1024


def _cparams(sem):
    return pltpu.CompilerParams(dimension_semantics=sem, vmem_limit_bytes=VMEM_LIMIT)


def _rms_scale(x, gain):
    ms = jnp.mean(x * x, axis=-1, keepdims=True)
    return x * lax.rsqrt(ms + NORM_EPS) * gain


def _ffn_kernel(x_ref, gain_ref, wg_ref, wu_ref, wd_ref, o_ref, h_ref):
    j = pl.program_id(1)

    @pl.when(j == 0)
    def _():
        x = x_ref[...]
        h_ref[...] = _rms_scale(x, gain_ref[...]).astype(BF16)
        o_ref[...] = x

    h = h_ref[...]
    g = jnp.dot(h, wg_ref[...], preferred_element_type=F32)
    u = jnp.dot(h, wu_ref[...], preferred_element_type=F32)
    a = (g * jax.nn.sigmoid(g) * u * 0.5).astype(BF16)
    o_ref[...] += jnp.dot(a, wd_ref[...], preferred_element_type=F32)


def _ffn_call(x, gain, w_gu, w_down, *, tm, tf):
    t, d = x.shape
    f = w_down.shape[0]
    nf = f // tf
    return pl.pallas_call(
        _ffn_kernel,
        grid=(t // tm, nf),
        in_specs=[
            pl.BlockSpec((tm, d), lambda i, j: (i, 0)),
            pl.BlockSpec((1, d), lambda i, j: (0, 0)),
            pl.BlockSpec((d, tf), lambda i, j: (0, j)),
            pl.BlockSpec((d, tf), lambda i, j: (0, j + nf)),
            pl.BlockSpec((tf, d), lambda i, j: (j, 0)),
        ],
        out_specs=pl.BlockSpec((tm, d), lambda i, j: (i, 0)),
        out_shape=jax.ShapeDtypeStruct((t, d), F32),
        scratch_shapes=[pltpu.VMEM((tm, d), BF16)],
        compiler_params=_cparams(("parallel", "arbitrary")),
        name="ffn_half",
    )(x, gain, w_gu, w_gu, w_down)


def _proj_kernel(x_ref, gain_ref, w_ref, o_ref, h_ref):
    @pl.when(pl.program_id(1) == 0)
    def _():
        h_ref[...] = _rms_scale(x_ref[...], gain_ref[...]).astype(BF16)

    o_ref[...] = jnp.dot(h_ref[...], w_ref[...], preferred_element_type=F32)


def _proj_call(x, gain, w, *, tm, tn):
    t, d = x.shape
    n = w.shape[1]
    return pl.pallas_call(
        _proj_kernel,
        grid=(t // tm, n // tn),
        in_specs=[
            pl.BlockSpec((tm, d), lambda i, j: (i, 0)),
            pl.BlockSpec((1, d), lambda i, j: (0, 0)),
            pl.BlockSpec((d, tn), lambda i, j: (0, j)),
        ],
        out_specs=pl.BlockSpec((tm, tn), lambda i, j: (i, j)),
        out_shape=jax.ShapeDtypeStruct((t, n), F32),
        scratch_shapes=[pltpu.VMEM((tm, d), BF16)],
        compiler_params=_cparams(("parallel", "arbitrary")),
        name="mixer_proj",
    )(x, gain, w)


def _mixout_kernel(x_ref, fm_ref, ob_ref, ga_ref, gb_ref, wfo_ref, who_ref, wo_ref, o_ref, m_ref):
    @pl.when(pl.program_id(1) == 0)
    def _():
        a = jnp.dot(fm_ref[...].astype(BF16), wfo_ref[...], preferred_element_type=F32)
        b = jnp.dot(ob_ref[...], who_ref[...], preferred_element_type=F32)
        m = jax.nn.sigmoid(ga_ref[...]) * a + jax.nn.sigmoid(gb_ref[...]) * b
        m_ref[...] = m.astype(BF16)

    o_ref[...] = x_ref[...] + jnp.dot(m_ref[...], wo_ref[...], preferred_element_type=F32)


def _mixout_call(x, fm, ob, proj, w_fo, w_ho, w_out, *, tm, tn, gate_a_col):
    t, d = x.shape
    ga_blk = gate_a_col // d
    return pl.pallas_call(
        _mixout_kernel,
        grid=(t // tm, d // tn),
        in_specs=[
            pl.BlockSpec((tm, tn), lambda i, j: (i, j)),
            pl.BlockSpec((tm, fm.shape[1]), lambda i, j: (i, 0)),
            pl.BlockSpec((tm, ob.shape[1]), lambda i, j: (i, 0)),
            pl.BlockSpec((tm, d), lambda i, j: (i, ga_blk)),
            pl.BlockSpec((tm, d), lambda i, j: (i, ga_blk + 1)),
            pl.BlockSpec(w_fo.shape, lambda i, j: (0, 0)),
            pl.BlockSpec(w_ho.shape, lambda i, j: (0, 0)),
            pl.BlockSpec((d, tn), lambda i, j: (0, j)),
        ],
        out_specs=pl.BlockSpec((tm, tn), lambda i, j: (i, j)),
        out_shape=jax.ShapeDtypeStruct((t, d), F32),
        scratch_shapes=[pltpu.VMEM((tm, d), BF16)],
        compiler_params=_cparams(("parallel", "arbitrary")),
        name="mixer_out",
    )(x, fm, ob, proj, proj, w_fo, w_ho, w_out)


def _norm_kernel(x_ref, gain_ref, o_ref):
    o_ref[...] = _rms_scale(x_ref[...], gain_ref[...])


def _norm_call(x, gain, *, tm):
    t, d = x.shape
    return pl.pallas_call(
        _norm_kernel,
        grid=(t // tm,),
        in_specs=[pl.BlockSpec((tm, d), lambda i: (i, 0)), pl.BlockSpec((1, d), lambda i: (0, 0))],
        out_specs=pl.BlockSpec((tm, d), lambda i: (i, 0)),
        out_shape=jax.ShapeDtypeStruct((t, d), F32),
        compiler_params=_cparams(("parallel",)),
        name="final_norm",
    )(x, gain)


def _lower_bound_kernel(logit_ref, o_ref, *, depth):
    for d0 in range(0, logit_ref.shape[0], depth):
        rows = [logit_ref[d0 + l:d0 + l + 1, :] for l in range(depth)]
        mx = functools.reduce(jnp.maximum, rows)
        es = [jnp.exp(r - mx) for r in rows]
        tot = functools.reduce(lambda a, b: a + b, es)
        ps = [e / tot for e in es]
        acc = ps[0]
        for l in range(depth):
            if l:
                acc = acc + ps[l]
            o_ref[d0 + l:d0 + l + 1, :] = jnp.maximum(acc - ps[0], 0.0)


def _lower_bound_call(lb_logits):
    dirs, depth, hw = lb_logits.shape
    lb = pl.pallas_call(
        functools.partial(_lower_bound_kernel, depth=depth),
        out_shape=jax.ShapeDtypeStruct((dirs * depth, hw), F32),
        name="lower_bounds",
    )(lb_logits.astype(F32).reshape(dirs * depth, hw))
    return lb.reshape(dirs, depth, hw)


def _split2_np(m):
    hi = m.astype(ml_dtypes.bfloat16).astype(np.float64)
    lo = m - hi
    return hi, lo


def _pad_to(m, axis, mult):
    pad = -m.shape[axis] % mult
    widths = [(0, 0)] * m.ndim
    widths[axis] = (0, pad)
    return np.pad(m, widths)


def _split2(x):
    hi = x.astype(BF16)
    lo = (x - hi.astype(F32)).astype(BF16)
    return hi, lo


def _dft_constants(s, n1, n2, gd, lanes):
    k1 = np.arange(n1, dtype=np.float64)
    c1 = np.cos(2 * np.pi * np.outer(k1, k1) / n1)
    s1 = np.sin(2 * np.pi * np.outer(k1, k1) / n1)
    m1 = np.concatenate([c1, -s1], axis=0)
    m1h, m1l = _split2_np(m1)
    m1cat = _pad_to(np.concatenate([m1h, m1h, m1l], axis=1), 1, LANES)

    k2 = np.arange(n2, dtype=np.float64)
    c2 = np.cos(2 * np.pi * np.outer(k2, k2) / n2)
    s2 = np.sin(2 * np.pi * np.outer(k2, k2) / n2)
    m2 = np.block([[c2, s2], [-s2, c2]])
    m2h, m2l = _split2_np(m2)
    m2cat = np.concatenate([m2h, m2h, m2l], axis=1)

    kc = np.arange(gd, dtype=np.float64)
    scale = 1.0 / np.sqrt(float(s) * gd)
    cc = np.cos(2 * np.pi * np.outer(kc, kc) / gd) * scale
    sc = np.sin(2 * np.pi * np.outer(kc, kc) / gd) * scale
    mc = np.concatenate([cc, sc], axis=0)
    mch, mcl = _split2_np(mc)
    mccat = np.concatenate([mch, mch, mcl], axis=0)

    ang = 2 * np.pi * np.outer(k1, np.arange(n2, dtype=np.float64)) / s
    twc = jnp.repeat(jnp.asarray(np.cos(ang), F32), lanes, axis=1)
    tws = jnp.repeat(jnp.asarray(np.sin(ang), F32), lanes, axis=1)
    return (jnp.asarray(m1cat, BF16), jnp.asarray(m2cat, BF16), jnp.asarray(mccat, BF16), twc, tws)


def _dft1_kernel(x_ref, m1_ref, twc_ref, tws_ref, o_ref):
    n1 = x_ref.shape[1]
    xh, xl = _split2(x_ref[0])
    pieces = [xh, xl, xh]
    k_pad = m1_ref.shape[1] - 3 * n1
    if k_pad:
        pieces.append(jnp.zeros((k_pad, xh.shape[1]), BF16))
    t = jnp.dot(m1_ref[...], jnp.concatenate(pieces, axis=0), preferred_element_type=F32)
    tr, ti = t[:n1], t[n1:]
    c, s = twc_ref[...], tws_ref[...]
    o_ref[0, 0] = tr * c + ti * s
    o_ref[0, 1] = ti * c - tr * s


def _dft2_kernel(t_ref, m2_ref, mc_ref, o_ref):
    n2 = t_ref.shape[3]
    gd = mc_ref.shape[1]
    t = jnp.concatenate([t_ref[0, 0, 0], t_ref[0, 1, 0]], axis=0)
    th, tl = _split2(t)
    g = jnp.dot(m2_ref[...], jnp.concatenate([th, tl, th], axis=0), preferred_element_type=F32)
    gr, gi = g[:n2], g[n2:]
    for grp in range(t.shape[1] // gd):
        sl = slice(grp * gd, (grp + 1) * gd)
        gg = jnp.concatenate([gr[:, sl], gi[:, sl]], axis=1)
        gh, gl = _split2(gg)
        o_ref[0, :, sl] = jnp.dot(jnp.concatenate([gh, gl, gh], axis=1), mc_ref[...],
                                  preferred_element_type=F32)


def _fourier_call(uf, batch, seq, consts, *, lane_chunk):
    w = uf.shape[1]
    n2 = DFT_INNER
    n1 = seq // n2
    m1cat, m2cat, mccat, twc, tws = consts
    x = uf.reshape(batch, n1, n2 * w)
    tw = pl.pallas_call(
        _dft1_kernel,
        grid=(batch, n2 * w // lane_chunk),
        in_specs=[
            pl.BlockSpec((1, n1, lane_chunk), lambda b, j: (b, 0, j)),
            pl.BlockSpec(m1cat.shape, lambda b, j: (0, 0)),
            pl.BlockSpec((n1, lane_chunk), lambda b, j: (0, j)),
            pl.BlockSpec((n1, lane_chunk), lambda b, j: (0, j)),
        ],
        out_specs=pl.BlockSpec((1, 2, n1, lane_chunk), lambda b, j: (b, 0, 0, j)),
        out_shape=jax.ShapeDtypeStruct((batch, 2, n1, n2 * w), F32),
        compiler_params=_cparams(("parallel", "parallel")),
        name="dft_stage1",
    )(x, m1cat, twc, tws)
    tw = tw.reshape(batch, 2, n1, n2, w)
    y = pl.pallas_call(
        _dft2_kernel,
        grid=(batch, n1),
        in_specs=[
            pl.BlockSpec((1, 2, 1, n2, w), lambda b, k: (b, 0, k, 0, 0)),
            pl.BlockSpec(m2cat.shape, lambda b, k: (0, 0)),
            pl.BlockSpec(mccat.shape, lambda b, k: (0, 0)),
        ],
        out_specs=pl.BlockSpec((1, n2, w), lambda b, k: (b, 0, k)),
        out_shape=jax.ShapeDtypeStruct((batch, n2, n1 * w), F32),
        compiler_params=_cparams(("parallel", "parallel")),
        name="dft_stage2",
    )(tw, m2cat, mccat)
    return y.reshape(batch * seq, w)


def _gates(z, lb):
    f = lb + (1.0 - lb) * jax.nn.sigmoid(z)
    log_f = jnp.log(jnp.maximum(f, MIN_FORGET))
    k = (1.0 - lb) * jax.nn.sigmoid(-z)
    return log_f, k


def _split3_cat(x):
    h1 = x.astype(BF16)
    r = x - h1.astype(F32)
    h2 = r.astype(BF16)
    r = r - h2.astype(F32)
    return jnp.concatenate([h1, h2, r.astype(BF16)], axis=1)


def _cumsum_rows(x, tri):
    w = x.shape[1]
    y = jnp.dot(tri, _split3_cat(x), preferred_element_type=F32)
    return y[:, :w] + y[:, w:2 * w] + y[:, 2 * w:]


def _boundary_rows(x, n, row8):
    c, w = x.shape
    parts = []
    if 2 * n >= SUBLANES:
        for p in range(c // (2 * n)):
            r = p * 2 * n + n - 1
            parts.append(jnp.broadcast_to(x[r:r + 1, :], (2 * n, w)))
    else:
        lo = row8 < 4
        for p in range(c // SUBLANES):
            r = p * SUBLANES
            a = jnp.broadcast_to(x[r + 1:r + 2, :], (SUBLANES, w))
            b = jnp.broadcast_to(x[r + 5:r + 6, :], (SUBLANES, w))
            parts.append(jnp.where(lo, a, b))
    return jnp.concatenate(parts, axis=0)


def _nt_dot(a, b):
    return lax.dot_general(a, b, (((1,), (1,)), ((), ())), preferred_element_type=F32)


def _intra_scores(q, kf, kb, lf, lfb, b_f, bx_b, b_b, row, row8, xor):
    c = q.shape[0]
    p = jnp.zeros((c, c), F32)
    n = 1
    while n < c:
        odd = (row & n) != 0
        if n == 1:
            e_q = jnp.where(odd, lf, lfb)
            qt = q * jnp.exp(e_q)
            kt = jnp.where(odd, kb, kf)
        else:
            d1 = b_f - _boundary_rows(b_f, n, row8)
            d2 = _boundary_rows(b_b, n, row8) - bx_b
            qt = q * jnp.exp(jnp.where(odd, d1, d2))
            kt = jnp.where(odd, kb, kf) * jnp.exp(-jnp.where(odd, d2, d1))
        pn = _nt_dot(qt.astype(BF16), kt.astype(BF16))
        p = jnp.where((xor >= n) & (xor < 2 * n), pn, p)
        n *= 2
    return p


def _scan_fwd_kernel(q_ref, zf_ref, zb_ref, v_ref, lbf_ref, lbb_ref, o_ref, st_ref, *, chunk):
    @pl.when(pl.program_id(2) == 0)
    def _():
        st_ref[...] = jnp.zeros_like(st_ref)

    c = chunk
    w = q_ref.shape[1]
    ri = lax.broadcasted_iota(jnp.int32, (c, c), 0)
    ci = lax.broadcasted_iota(jnp.int32, (c, c), 1)
    tri = (ri >= ci).astype(BF16)
    xor = ri ^ ci
    row = lax.broadcasted_iota(jnp.int32, (c, w), 0)
    row8 = lax.broadcasted_iota(jnp.int32, (SUBLANES, w), 0)
    lbf = lbf_ref[...]
    lbb = lbb_ref[...]

    for ch in range(q_ref.shape[0] // c):
        sl = pl.ds(ch * c, c)
        qr = q_ref[sl, :]
        q = qr * jax.nn.sigmoid(qr)
        v = v_ref[sl, :]
        lf, kf = _gates(zf_ref[sl, :], lbf)
        lfb, kb = _gates(zb_ref[sl, :], lbb)
        b_f = _cumsum_rows(lf, tri)
        b_b = _cumsum_rows(lfb, tri)
        bx_b = b_b - lfb

        p = _intra_scores(q, kf, kb, lf, lfb, b_f, bx_b, b_b, row, row8, xor)
        diag = jnp.sum(q * (kf + kb), axis=-1, keepdims=True)
        vb = v.astype(BF16)
        o = jnp.dot(p.astype(BF16), vb, preferred_element_type=F32) + diag * v

        st = st_ref[...]
        o = o + _nt_dot((q * jnp.exp(b_f)).astype(BF16), st.astype(BF16))
        o_ref[sl, :] = o

        b_last = b_f[c - 1:c, :]
        kh = (kf * jnp.exp(b_last - b_f)).astype(BF16)
        st_ref[...] = st * jnp.exp(b_last) + jnp.dot(v.T.astype(BF16), kh,
                                                     preferred_element_type=F32)


def _scan_bwd_kernel(q_ref, zb_ref, v_ref, g_ref, o1_ref, lbb_ref, hg_ref, o_ref, st_ref, *, chunk):
    @pl.when(pl.program_id(2) == 0)
    def _():
        st_ref[...] = jnp.zeros_like(st_ref)

    c = chunk
    ri = lax.broadcasted_iota(jnp.int32, (c, c), 0)
    ci = lax.broadcasted_iota(jnp.int32, (c, c), 1)
    tri = (ri >= ci).astype(BF16)
    lbb = lbb_ref[...]
    hg = hg_ref[...]

    for ch in reversed(range(q_ref.shape[0] // c)):
        sl = pl.ds(ch * c, c)
        qr = q_ref[sl, :]
        q = qr * jax.nn.sigmoid(qr)
        v = v_ref[sl, :]
        lfb, kb = _gates(zb_ref[sl, :], lbb)
        b_b = _cumsum_rows(lfb, tri)
        bx_b = b_b - lfb
        tot = b_b[c - 1:c, :]

        st = st_ref[...]
        o = o1_ref[sl, :] + _nt_dot((q * jnp.exp(tot - bx_b)).astype(BF16), st.astype(BF16))
        kh = (kb * jnp.exp(bx_b)).astype(BF16)
        st_ref[...] = st * jnp.exp(tot) + jnp.dot(v.T.astype(BF16), kh,
                                                  preferred_element_type=F32)

        o = o * lax.rsqrt(jnp.mean(o * o, axis=-1, keepdims=True) + NORM_EPS)
        gr = g_ref[sl, :]
        o_ref[sl, :] = (o * hg * (gr * jax.nn.sigmoid(gr))).astype(BF16)


def _scan_call(proj, lb_f, lb_b, hg_gain, *, row0, batch, seq, heads, cols, tc):
    hd = HEAD_DIM
    nb = seq // tc
    rb0 = row0 // tc
    cq, czf, czb, cv, cg = (c // hd for c in cols)

    def tok(col):
        return pl.BlockSpec((tc, hd), lambda b, h, j: (rb0 + b * nb + j, col + h))

    def tok_rev(col):
        return pl.BlockSpec((tc, hd), lambda b, h, j: (rb0 + b * nb + nb - 1 - j, col + h))

    par = pl.BlockSpec((1, hd), lambda b, h, j: (0, h))
    o1 = pl.pallas_call(
        functools.partial(_scan_fwd_kernel, chunk=SCAN_CHUNK),
        grid=(batch, heads, nb),
        in_specs=[tok(cq), tok(czf), tok(czb), tok(cv), par, par],
        out_specs=pl.BlockSpec((tc, hd), lambda b, h, j: (b * nb + j, h)),
        out_shape=jax.ShapeDtypeStruct((batch * seq, heads * hd), F32),
        scratch_shapes=[pltpu.VMEM((hd, hd), F32)],
        compiler_params=_cparams(("parallel", "parallel", "arbitrary")),
        name="hgrn2_fwd",
    )(proj, proj, proj, proj, lb_f, lb_b)
    return pl.pallas_call(
        functools.partial(_scan_bwd_kernel, chunk=SCAN_CHUNK),
        grid=(batch, heads, nb),
        in_specs=[tok_rev(cq), tok_rev(czb), tok_rev(cv), tok_rev(cg),
                  pl.BlockSpec((tc, hd), lambda b, h, j: (b * nb + nb - 1 - j, h)), par, par],
        out_specs=pl.BlockSpec((tc, hd), lambda b, h, j: (b * nb + nb - 1 - j, h)),
        out_shape=jax.ShapeDtypeStruct((batch * seq, heads * hd), BF16),
        scratch_shapes=[pltpu.VMEM((hd, hd), F32)],
        compiler_params=_cparams(("parallel", "parallel", "arbitrary")),
        name="hgrn2_bwd",
    )(proj, proj, proj, proj, o1, lb_b, hg_gain)


def _pad_cols(w, n):
    return jnp.pad(w, ((0, 0), (0, n - w.shape[1])))


def _trunk(x, groups, params, *, ffn_tm, ffn_tf, proj_tm, proj_tn, out_tm, out_tn, scan_tc,
           dft_lanes):
    (ffn1_norm, ffn1_w_in, ffn1_w_down, mix_norm, w_in, lb_logits, hg_norm, w_fourier_out,
     w_hgrn_out, w_out, ffn2_norm, ffn2_w_in, ffn2_w_down, final_norm) = params
    depth, d = mix_norm.shape
    fw = w_fourier_out.shape[1]
    hw = w_hgrn_out.shape[1]
    heads = hw // HEAD_DIM
    d_ff = ffn1_w_down.shape[1]
    f_pad = -(-d_ff // ffn_tf) * ffn_tf
    cols = tuple(fw + j * hw for j in (0, 1, 2, 3, 4))
    gate_a_col = fw + 5 * hw

    def ffn_weights(w_in_l, w_down_l):
        gate, up = w_in_l[:, :d_ff], w_in_l[:, d_ff:]
        w_gu = jnp.concatenate([_pad_cols(gate, f_pad), _pad_cols(up, f_pad)], axis=1).astype(BF16)
        w_dn = jnp.pad(w_down_l, ((0, f_pad - d_ff), (0, 0))).astype(BF16)
        return w_gu, w_dn

    lb = _lower_bound_call(lb_logits)
    dft = {seq: _dft_constants(seq, seq // DFT_INNER, DFT_INNER, FOURIER_GROUP_DIM, fw)
           for (_, _, seq) in groups}

    for l in range(depth):
        w_gu, w_dn = ffn_weights(ffn1_w_in[l], ffn1_w_down[l])
        x = _ffn_call(x, ffn1_norm[l][None, :], w_gu, w_dn, tm=ffn_tm, tf=ffn_tf)

        proj = _proj_call(x, mix_norm[l][None, :], w_in[l].astype(BF16), tm=proj_tm, tn=proj_tn)
        uf = proj[:, :fw]
        fm_parts, ob_parts = [], []
        for (row0, batch, seq) in groups:
            fm_parts.append(_fourier_call(uf[row0:row0 + batch * seq], batch, seq, dft[seq],
                                          lane_chunk=dft_lanes))
            ob_parts.append(_scan_call(proj, lb[0, l][None, :], lb[1, l][None, :],
                                       hg_norm[l][None, :], row0=row0, batch=batch, seq=seq,
                                       heads=heads, cols=cols, tc=scan_tc))
        fm = jnp.concatenate(fm_parts, axis=0)
        ob = jnp.concatenate(ob_parts, axis=0)
        x = _mixout_call(x, fm, ob, proj, w_fourier_out[l].astype(BF16),
                         w_hgrn_out[l].astype(BF16), w_out[l].astype(BF16),
                         tm=out_tm, tn=out_tn, gate_a_col=gate_a_col)

        w_gu, w_dn = ffn_weights(ffn2_w_in[l], ffn2_w_down[l])
        x = _ffn_call(x, ffn2_norm[l][None, :], w_gu, w_dn, tm=ffn_tm, tf=ffn_tf)

    return _norm_call(x, final_norm[None, :], tm=ffn_tm)


def kernel(x_prompt, x_sample, ffn1_norm, ffn1_w_in, ffn1_w_down, mix_norm, w_in, lb_logits, hg_norm, w_fourier_out, w_hgrn_out, w_out, ffn2_norm, ffn2_w_in, ffn2_w_down, final_norm):
    bp, sp, d = x_prompt.shape
    bs, ss, _ = x_sample.shape
    x = jnp.concatenate([x_prompt.reshape(bp * sp, d), x_sample.reshape(bs * ss, d)], axis=0)
    groups = [(0, bp, sp), (bp * sp, bs, ss)]
    params = (ffn1_norm, ffn1_w_in, ffn1_w_down, mix_norm, w_in, lb_logits, hg_norm,
              w_fourier_out, w_hgrn_out, w_out, ffn2_norm, ffn2_w_in, ffn2_w_down, final_norm)
    y = _trunk(x, groups, params, ffn_tm=512, ffn_tf=512, proj_tm=1024, proj_tn=512,
               out_tm=512, out_tn=512, scan_tc=512, dft_lanes=8192)
    return (y[:bp * sp].reshape(bp, sp, d), y[bp * sp:].reshape(bs, ss, d))
```

```python
import functools

import ml_dtypes
import numpy as np
import jax
import jax.numpy as jnp
from jax import lax
from jax.experimental import pallas as pl
from jax.experimental.pallas import tpu as pltpu

F32 = jnp.float32
BF16 = jnp.bfloat16

NORM_EPS = 1e-6
MIN_FORGET = 1e-30
LANES = 128
SUBLANES = 8
HEAD_DIM = 128
FOURIER_GROUP_DIM = 128
SCAN_CHUNK = 128
DFT_INNER = 128
VMEM_LIMIT = 56 * 1024 * 1024


def _cparams(sem):
    return pltpu.CompilerParams(dimension_semantics=sem, vmem_limit_bytes=VMEM_LIMIT)


def _rms_scale(x, gain):
    ms = jnp.mean(x * x, axis=-1, keepdims=True)
    return x * lax.rsqrt(ms + NORM_EPS) * gain


def _ffn_kernel(x_ref, gain_ref, wg_ref, wu_ref, wd_ref, o_ref, h_ref):
    j = pl.program_id(1)

    @pl.when(j == 0)
    def _():
        x = x_ref[...]
        h_ref[...] = _rms_scale(x, gain_ref[...]).astype(BF16)
        o_ref[...] = x

    h = h_ref[...]
    g = jnp.dot(h, wg_ref[...], preferred_element_type=F32)
    u = jnp.dot(h, wu_ref[...], preferred_element_type=F32)
    a = (g * jax.nn.sigmoid(g) * u * 0.5).astype(BF16)
    o_ref[...] += jnp.dot(a, wd_ref[...], preferred_element_type=F32)


def _ffn_call(x, gain, w_gu, w_down, *, tm, tf):
    t, d = x.shape
    f = w_down.shape[0]
    nf = f // tf
    return pl.pallas_call(
        _ffn_kernel,
        grid=(t // tm, nf),
        in_specs=[
            pl.BlockSpec((tm, d), lambda i, j: (i, 0)),
            pl.BlockSpec((1, d), lambda i, j: (0, 0)),
            pl.BlockSpec((d, tf), lambda i, j: (0, j)),
            pl.BlockSpec((d, tf), lambda i, j: (0, j + nf)),
            pl.BlockSpec((tf, d), lambda i, j: (j, 0)),
        ],
        out_specs=pl.BlockSpec((tm, d), lambda i, j: (i, 0)),
        out_shape=jax.ShapeDtypeStruct((t, d), F32),
        scratch_shapes=[pltpu.VMEM((tm, d), BF16)],
        compiler_params=_cparams(("parallel", "arbitrary")),
        name="ffn_half",
    )(x, gain, w_gu, w_gu, w_down)


def _proj_kernel(x_ref, gain_ref, w_ref, uf_ref, mid_ref, gate_ref, h_ref, *, j_mid, j_gate):
    j = pl.program_id(1)

    @pl.when(j == 0)
    def _():
        h_ref[...] = _rms_scale(x_ref[...], gain_ref[...]).astype(BF16)

    y = jnp.dot(h_ref[...], w_ref[...], preferred_element_type=F32)

    @pl.when(j < j_mid)
    def _():
        uf_ref[...] = y

    @pl.when((j >= j_mid) & (j < j_gate))
    def _():
        mid_ref[...] = y

    @pl.when(j >= j_gate)
    def _():
        gate_ref[...] = y


def _proj_call(x, gain, w, *, tm, tn, n_fourier, n_gates):
    t, d = x.shape
    n = w.shape[1]
    n_mid = n - n_fourier - n_gates
    j_mid = n_fourier // tn
    j_gate = (n_fourier + n_mid) // tn
    assert j_mid == 1 and n_fourier == tn and n_mid % tn == 0 and n_gates % tn == 0
    return pl.pallas_call(
        functools.partial(_proj_kernel, j_mid=j_mid, j_gate=j_gate),
        grid=(t // tm, n // tn),
        in_specs=[
            pl.BlockSpec((tm, d), lambda i, j: (i, 0)),
            pl.BlockSpec((1, d), lambda i, j: (0, 0)),
            pl.BlockSpec((d, tn), lambda i, j: (0, j)),
        ],
        out_specs=[
            pl.BlockSpec((tm, tn), lambda i, j: (i, 0)),
            pl.BlockSpec((tm, tn), lambda i, j: (i, jnp.clip(j - j_mid, 0, j_gate - j_mid - 1))),
            pl.BlockSpec((tm, tn), lambda i, j: (i, jnp.maximum(j - j_gate, 0))),
        ],
        out_shape=[jax.ShapeDtypeStruct((t, n_fourier), F32),
                   jax.ShapeDtypeStruct((t, n_mid), F32),
                   jax.ShapeDtypeStruct((t, n_gates), F32)],
        scratch_shapes=[pltpu.VMEM((tm, d), BF16)],
        compiler_params=_cparams(("parallel", "arbitrary")),
        name="mixer_proj",
    )(x, gain, w)


def _sigmoid(x):
    return 0.5 + 0.5 * jnp.tanh(0.5 * x)


def _mixout_kernel(*refs, tile_starts):
    ng = len(tile_starts)
    x_ref = refs[0]
    fm_refs = refs[1:1 + ng]
    ob_refs = refs[1 + ng:1 + 2 * ng]
    ga_ref, gb_ref, wfo_ref, who_ref, wo_ref, o_ref = refs[1 + 2 * ng:]
    i = pl.program_id(0)
    fm, ob = fm_refs[0][...], ob_refs[0][...]
    for g in range(1, ng):
        in_g = i >= tile_starts[g]
        fm = jnp.where(in_g, fm_refs[g][...], fm)
        ob = jnp.where(in_g, ob_refs[g][...], ob)
    a = jnp.dot(fm.astype(BF16), wfo_ref[...], preferred_element_type=F32)
    b = jnp.dot(ob, who_ref[...], preferred_element_type=F32)
    m = (_sigmoid(ga_ref[...]) * a + _sigmoid(gb_ref[...]) * b).astype(BF16)
    o_ref[...] = x_ref[...] + jnp.dot(m, wo_ref[...], preferred_element_type=F32)


def _mixout_call(x, fm_parts, ob_parts, gates, w_fo, w_ho, w_out, *, tm):
    t, d = x.shape
    tile_starts, tile_counts, start = [], [], 0
    for part in fm_parts:
        tile_starts.append(start)
        tile_counts.append(part.shape[0] // tm)
        start += part.shape[0] // tm

    def group_spec(part, g):
        return pl.BlockSpec(
            (tm, part.shape[1]),
            lambda i: (jnp.clip(i - tile_starts[g], 0, tile_counts[g] - 1), 0))

    def resident(w):
        return pl.BlockSpec(w.shape, lambda i: (0, 0), pipeline_mode=pl.Buffered(1))

    return pl.pallas_call(
        functools.partial(_mixout_kernel, tile_starts=tuple(tile_starts)),
        grid=(t // tm,),
        in_specs=(
            [pl.BlockSpec((tm, d), lambda i: (i, 0))]
            + [group_spec(p, g) for g, p in enumerate(fm_parts)]
            + [group_spec(p, g) for g, p in enumerate(ob_parts)]
            + [pl.BlockSpec((tm, d), lambda i: (i, 0)), pl.BlockSpec((tm, d), lambda i: (i, 1)),
               resident(w_fo), resident(w_ho), resident(w_out)]),
        out_specs=pl.BlockSpec((tm, d), lambda i: (i, 0)),
        out_shape=jax.ShapeDtypeStruct((t, d), F32),
        compiler_params=_cparams(("parallel",)),
        name="mixer_out",
    )(x, *fm_parts, *ob_parts, gates, gates, w_fo, w_ho, w_out)


def _norm_kernel(x_ref, gain_ref, o_ref):
    o_ref[...] = _rms_scale(x_ref[...], gain_ref[...])


def _norm_call(x, gain, *, tm):
    t, d = x.shape
    return pl.pallas_call(
        _norm_kernel,
        grid=(t // tm,),
        in_specs=[pl.BlockSpec((tm, d), lambda i: (i, 0)), pl.BlockSpec((1, d), lambda i: (0, 0))],
        out_specs=pl.BlockSpec((tm, d), lambda i: (i, 0)),
        out_shape=jax.ShapeDtypeStruct((t, d), F32),
        compiler_params=_cparams(("parallel",)),
        name="final_norm",
    )(x, gain)


def _lower_bound_kernel(logit_ref, o_ref, *, depth):
    for d0 in range(0, logit_ref.shape[0], depth):
        rows = [logit_ref[d0 + l:d0 + l + 1, :] for l in range(depth)]
        mx = functools.reduce(jnp.maximum, rows)
        es = [jnp.exp(r - mx) for r in rows]
        tot = functools.reduce(lambda a, b: a + b, es)
        ps = [e / tot for e in es]
        acc = ps[0]
        for l in range(depth):
            if l:
                acc = acc + ps[l]
            o_ref[d0 + l:d0 + l + 1, :] = jnp.maximum(acc - ps[0], 0.0)


def _lower_bound_call(lb_logits):
    dirs, depth, hw = lb_logits.shape
    lb = pl.pallas_call(
        functools.partial(_lower_bound_kernel, depth=depth),
        out_shape=jax.ShapeDtypeStruct((dirs * depth, hw), F32),
        name="lower_bounds",
    )(lb_logits.astype(F32).reshape(dirs * depth, hw))
    return lb.reshape(dirs, depth, hw)


def _split2_np(m):
    hi = m.astype(ml_dtypes.bfloat16).astype(np.float64)
    lo = m - hi
    return hi, lo


def _pad_to(m, axis, mult):
    pad = -m.shape[axis] % mult
    widths = [(0, 0)] * m.ndim
    widths[axis] = (0, pad)
    return np.pad(m, widths)


def _split2(x):
    hi = x.astype(BF16)
    lo = (x - hi.astype(F32)).astype(BF16)
    return hi, lo


def _dft_constants(s, n1, n2, gd, lanes):
    k1 = np.arange(n1, dtype=np.float64)
    c1 = np.cos(2 * np.pi * np.outer(k1, k1) / n1)
    s1 = np.sin(2 * np.pi * np.outer(k1, k1) / n1)
    m1 = np.concatenate([c1, -s1], axis=0)
    m1h, m1l = _split2_np(m1)
    m1cat = _pad_to(np.concatenate([m1h, m1h, m1l], axis=1), 1, LANES)

    k2 = np.arange(n2, dtype=np.float64)
    c2 = np.cos(2 * np.pi * np.outer(k2, k2) / n2)
    s2 = np.sin(2 * np.pi * np.outer(k2, k2) / n2)
    m2 = np.block([[c2, s2], [-s2, c2]])
    m2h, m2l = _split2_np(m2)
    m2cat = np.concatenate([m2h, m2h, m2l], axis=1)

    kc = np.arange(gd, dtype=np.float64)
    scale = 1.0 / np.sqrt(float(s) * gd)
    cc = np.cos(2 * np.pi * np.outer(kc, kc) / gd) * scale
    sc = np.sin(2 * np.pi * np.outer(kc, kc) / gd) * scale
    mc = np.concatenate([cc, sc], axis=0)
    mch, mcl = _split2_np(mc)
    mccat = np.concatenate([mch, mch, mcl], axis=0)

    ang = 2 * np.pi * np.outer(k1, np.arange(n2, dtype=np.float64)) / s
    twc = jnp.repeat(jnp.asarray(np.cos(ang), F32), lanes, axis=1)
    tws = jnp.repeat(jnp.asarray(np.sin(ang), F32), lanes, axis=1)
    return (jnp.asarray(m1cat, BF16), jnp.asarray(m2cat, BF16), jnp.asarray(mccat, BF16), twc, tws)


def _dft1_kernel(x_ref, m1_ref, twc_ref, tws_ref, o_ref):
    n1 = x_ref.shape[1]
    xh, xl = _split2(x_ref[0])
    pieces = [xh, xl, xh]
    k_pad = m1_ref.shape[1] - 3 * n1
    if k_pad:
        pieces.append(jnp.zeros((k_pad, xh.shape[1]), BF16))
    t = jnp.dot(m1_ref[...], jnp.concatenate(pieces, axis=0), preferred_element_type=F32)
    tr, ti = t[:n1], t[n1:]
    c, s = twc_ref[...], tws_ref[...]
    o_ref[0, 0] = tr * c + ti * s
    o_ref[0, 1] = ti * c - tr * s


def _dft2_kernel(t_ref, m2_ref, mc_ref, o_ref):
    n2 = t_ref.shape[3]
    gd = mc_ref.shape[1]
    t = jnp.concatenate([t_ref[0, 0, 0], t_ref[0, 1, 0]], axis=0)
    th, tl = _split2(t)
    g = jnp.dot(m2_ref[...], jnp.concatenate([th, tl, th], axis=0), preferred_element_type=F32)
    gr, gi = g[:n2], g[n2:]
    for grp in range(t.shape[1] // gd):
        sl = slice(grp * gd, (grp + 1) * gd)
        gg = jnp.concatenate([gr[:, sl], gi[:, sl]], axis=1)
        gh, gl = _split2(gg)
        o_ref[0, :, sl] = jnp.dot(jnp.concatenate([gh, gl, gh], axis=1), mc_ref[...],
                                  preferred_element_type=F32)


def _fourier_call(uf, batch, seq, consts, *, lane_chunk):
    w = uf.shape[1]
    n2 = DFT_INNER
    n1 = seq // n2
    m1cat, m2cat, mccat, twc, tws = consts
    x = uf.reshape(batch, n1, n2 * w)
    tw = pl.pallas_call(
        _dft1_kernel,
        grid=(batch, n2 * w // lane_chunk),
        in_specs=[
            pl.BlockSpec((1, n1, lane_chunk), lambda b, j: (b, 0, j)),
            pl.BlockSpec(m1cat.shape, lambda b, j: (0, 0)),
            pl.BlockSpec((n1, lane_chunk), lambda b, j: (0, j)),
            pl.BlockSpec((n1, lane_chunk), lambda b, j: (0, j)),
        ],
        out_specs=pl.BlockSpec((1, 2, n1, lane_chunk), lambda b, j: (b, 0, 0, j)),
        out_shape=jax.ShapeDtypeStruct((batch, 2, n1, n2 * w), F32),
        compiler_params=_cparams(("parallel", "parallel")),
        name="dft_stage1",
    )(x, m1cat, twc, tws)
    tw = tw.reshape(batch, 2, n1, n2, w)
    y = pl.pallas_call(
        _dft2_kernel,
        grid=(batch, n1),
        in_specs=[
            pl.BlockSpec((1, 2, 1, n2, w), lambda b, k: (b, 0, k, 0, 0)),
            pl.BlockSpec(m2cat.shape, lambda b, k: (0, 0)),
            pl.BlockSpec(mccat.shape, lambda b, k: (0, 0)),
        ],
        out_specs=pl.BlockSpec((1, n2, w), lambda b, k: (b, 0, k)),
        out_shape=jax.ShapeDtypeStruct((batch, n2, n1 * w), F32),
        compiler_params=_cparams(("parallel", "parallel")),
        name="dft_stage2",
    )(tw, m2cat, mccat)
    return y.reshape(batch * seq, w)


def _silu(x):
    h = 0.5 * x
    return h + h * jnp.tanh(h)


def _gate_consts(lb):
    c1 = 0.5 * (1.0 - lb)
    return lb + c1, c1


def _gates(z, c0, c1):
    f = c0 + c1 * jnp.tanh(0.5 * z)
    f_floor = jnp.maximum(f, MIN_FORGET)
    return f_floor, jnp.log(f_floor), 1.0 - f


def _cumsum_rows(x, tri):
    w = x.shape[1]
    hi = x.astype(BF16)
    lo = (x - hi.astype(F32)).astype(BF16)
    y = jnp.dot(tri, jnp.concatenate([hi, lo], axis=1), preferred_element_type=F32)
    return y[:, :w] + y[:, w:]


def _boundary_rows(x, n, row8):
    c, w = x.shape
    parts = []
    if 2 * n >= SUBLANES:
        for p in range(c // (2 * n)):
            r = p * 2 * n + n - 1
            parts.append(jnp.broadcast_to(x[r:r + 1, :], (2 * n, w)))
    else:
        lo = row8 < 4
        for p in range(c // SUBLANES):
            r = p * SUBLANES
            a = jnp.broadcast_to(x[r + 1:r + 2, :], (SUBLANES, w))
            b = jnp.broadcast_to(x[r + 5:r + 6, :], (SUBLANES, w))
            parts.append(jnp.where(lo, a, b))
    return jnp.concatenate(parts, axis=0)


def _nt_dot(a, b):
    return lax.dot_general(a, b, (((1,), (1,)), ((), ())), preferred_element_type=F32)


def _intra_scores(q, kf, kb, ff, fb, b_f, bx_b, b_b, row, row8, xor):
    c = q.shape[0]
    products = []
    n = c // 2
    while n >= 1:
        odd = (row & n) != 0
        kk = jnp.where(odd, kb, kf)
        if n == 1:
            qt = q * jnp.where(odd, ff, fb)
            kt = kk
        else:
            d1 = b_f - _boundary_rows(b_f, n, row8)
            d2 = _boundary_rows(b_b, n, row8) - bx_b
            qt = q * jnp.exp(jnp.minimum(d1, d2))
            kt = kk * jnp.exp(-jnp.maximum(d1, d2))
        products.append((n, _nt_dot(qt.astype(BF16), kt.astype(BF16))))
        n //= 2
    return products


def _assemble_scores(products, xor):
    p = None
    for n, pn in products:
        p = pn if p is None else jnp.where(xor < 2 * n, pn, p)
    return jnp.where(xor == 0, 0.0, p)


def _scan_fwd_kernel(q_ref, zf_ref, zb_ref, v_ref, lbf_ref, lbb_ref,
                     o_ref, qb_ref, kb_ref, tot_ref, vt_ref, st_ref, *, chunk):
    @pl.when(pl.program_id(2) == 0)
    def _():
        st_ref[...] = jnp.zeros_like(st_ref)

    c = chunk
    w = q_ref.shape[1]
    ri = lax.broadcasted_iota(jnp.int32, (c, c), 0)
    ci = lax.broadcasted_iota(jnp.int32, (c, c), 1)
    tri = (ri >= ci).astype(BF16)
    xor = ri ^ ci
    row = lax.broadcasted_iota(jnp.int32, (c, w), 0)
    row8 = lax.broadcasted_iota(jnp.int32, (SUBLANES, w), 0)
    c0f, c1f = _gate_consts(lbf_ref[...])
    c0b, c1b = _gate_consts(lbb_ref[...])

    nch = q_ref.shape[0] // c
    work = [dict(sl=pl.ds(ch * c, c), ch=ch) for ch in range(nch)]

    def stage_gates(s):
        sl = s["sl"]
        s["q"] = _silu(q_ref[sl, :])
        s["v"] = v_ref[sl, :]
        s["ff"], lf, s["kf"] = _gates(zf_ref[sl, :], c0f, c1f)
        s["fb"], s["lfb"], s["kb"] = _gates(zb_ref[sl, :], c0b, c1b)
        s["b_f"] = _cumsum_rows(lf, tri)
        s["b_b"] = _cumsum_rows(s["lfb"], tri)

    def stage_levels(s):
        q, v, kf, kb, b_f, b_b = s["q"], s["v"], s["kf"], s["kb"], s["b_f"], s["b_b"]
        bx_b = b_b - s["lfb"]
        s["products"] = _intra_scores(q, kf, kb, s["ff"], s["fb"], b_f, bx_b, b_b, row, row8, xor)
        s["diag"] = jnp.sum(q * (kf + kb), axis=-1, keepdims=True)
        b_last = b_f[c - 1:c, :]
        s["qh"] = (q * jnp.exp(b_f)).astype(BF16)
        kh = (kf * jnp.exp(b_last - b_f)).astype(BF16)
        vt = v.T.astype(BF16)
        vt_ref[:, s["sl"]] = vt
        s["upd"] = jnp.dot(vt, kh, preferred_element_type=F32)
        s["decay"] = jnp.exp(b_last)
        tot = b_b[c - 1:c, :]
        qb_ref[s["sl"], :] = (q * jnp.exp(tot - bx_b)).astype(BF16)
        kb_ref[s["sl"], :] = (kb * jnp.exp(bx_b)).astype(BF16)
        tot_ref[pl.ds(s["ch"] * SUBLANES, SUBLANES), :] = jnp.broadcast_to(jnp.exp(tot),
                                                                            (SUBLANES, w))

    def stage_scores(s):
        p = _assemble_scores(s.pop("products"), xor)
        s["pv"] = jnp.dot(p.astype(BF16), s["v"].astype(BF16), preferred_element_type=F32)

    stages = (stage_gates, stage_levels, stage_scores)
    for step in range(nch + len(stages) - 1):
        for k, stage in enumerate(stages):
            if 0 <= step - k < nch:
                stage(work[step - k])

    st = st_ref[...]
    for s in work:
        o_ref[s["sl"], :] = s["pv"] + s["diag"] * s["v"] + _nt_dot(s["qh"], st.astype(BF16))
        st = st * s["decay"] + s["upd"]
    st_ref[...] = st


def _scan_bwd_kernel(qb_ref, kb_ref, tot_ref, vt_ref, g_ref, o1_ref, hg_ref, o_ref, st_ref, *,
                     chunk, lookahead):
    @pl.when(pl.program_id(2) == 0)
    def _():
        st_ref[...] = jnp.zeros_like(st_ref)

    c = chunk
    nch = o1_ref.shape[0] // c
    hg = hg_ref[...]
    order = list(reversed(range(nch)))
    slices = {ch: pl.ds(ch * c, c) for ch in order}

    upd = {ch: jnp.dot(vt_ref[:, slices[ch]], kb_ref[slices[ch], :], preferred_element_type=F32)
           for ch in order}
    st = st_ref[...]
    st_in = {}
    for ch in order:
        st_in[ch] = st.astype(BF16)
        st = st * tot_ref[pl.ds(ch * SUBLANES, 1), :] + upd[ch]
    st_ref[...] = st

    inter = {}
    for i in range(nch + lookahead):
        if i < nch:
            ch = order[i]
            inter[ch] = _nt_dot(qb_ref[slices[ch], :], st_in.pop(ch))
        if i >= lookahead:
            ch = order[i - lookahead]
            o = o1_ref[slices[ch], :] + inter.pop(ch)
            o = o * lax.rsqrt(jnp.mean(o * o, axis=-1, keepdims=True) + NORM_EPS)
            o_ref[slices[ch], :] = (o * hg * _silu(g_ref[slices[ch], :])).astype(BF16)


def _scan_call(mid, lb_f, lb_b, hg_gain, *, row0, batch, seq, heads, cols, tc, tc_bwd):
    hd = HEAD_DIM
    nb = seq // tc
    rb0 = row0 // tc
    cq, czf, czb, cv, cg = (c // hd for c in cols)
    rows_per_tot = SCAN_CHUNK // SUBLANES
    hw = heads * hd

    def tok(col):
        return pl.BlockSpec((tc, hd), lambda b, h, j: (rb0 + b * nb + j, col + h))

    def own(rows):
        return pl.BlockSpec((rows, hd), lambda b, h, j: (b * nb + j, h))

    par = pl.BlockSpec((1, hd), lambda b, h, j: (0, h))
    o1, qb, kb, tot, vt = pl.pallas_call(
        functools.partial(_scan_fwd_kernel, chunk=SCAN_CHUNK),
        grid=(batch, heads, nb),
        in_specs=[tok(cq), tok(czf), tok(czb), tok(cv), par, par],
        out_specs=[own(tc), own(tc), own(tc), own(tc // rows_per_tot),
                   pl.BlockSpec((hd, tc), lambda b, h, j: (h, b * nb + j))],
        out_shape=[jax.ShapeDtypeStruct((batch * seq, hw), F32),
                   jax.ShapeDtypeStruct((batch * seq, hw), BF16),
                   jax.ShapeDtypeStruct((batch * seq, hw), BF16),
                   jax.ShapeDtypeStruct((batch * seq // rows_per_tot, hw), F32),
                   jax.ShapeDtypeStruct((hw, batch * seq), BF16)],
        scratch_shapes=[pltpu.VMEM((hd, hd), F32)],
        compiler_params=_cparams(("parallel", "parallel", "arbitrary")),
        name="hgrn2_fwd",
    )(mid, mid, mid, mid, lb_f, lb_b)

    nb2 = seq // tc_bwd
    rb2 = row0 // tc_bwd

    def own_rev(rows):
        return pl.BlockSpec((rows, hd), lambda b, h, j: (b * nb2 + nb2 - 1 - j, h))

    def tok_rev(col):
        return pl.BlockSpec((tc_bwd, hd), lambda b, h, j: (rb2 + b * nb2 + nb2 - 1 - j, col + h))

    return pl.pallas_call(
        functools.partial(_scan_bwd_kernel, chunk=SCAN_CHUNK, lookahead=2),
        grid=(batch, heads, nb2),
        in_specs=[own_rev(tc_bwd), own_rev(tc_bwd), own_rev(tc_bwd // rows_per_tot),
                  pl.BlockSpec((hd, tc_bwd), lambda b, h, j: (h, b * nb2 + nb2 - 1 - j)),
                  tok_rev(cg), own_rev(tc_bwd), par],
        out_specs=own_rev(tc_bwd),
        out_shape=jax.ShapeDtypeStruct((batch * seq, hw), BF16),
        scratch_shapes=[pltpu.VMEM((hd, hd), F32)],
        compiler_params=_cparams(("parallel", "parallel", "arbitrary")),
        name="hgrn2_bwd",
    )(qb, kb, tot, vt, mid, o1, hg_gain)


def _pad_cols(w, n):
    return jnp.pad(w, ((0, 0), (0, n - w.shape[1])))


def _trunk(x, groups, params, *, ffn_tm, ffn_tf, proj_tm, proj_tn, out_tm, scan_tc, scan_tc_bwd,
           dft_lanes):
    (ffn1_norm, ffn1_w_in, ffn1_w_down, mix_norm, w_in, lb_logits, hg_norm, w_fourier_out,
     w_hgrn_out, w_out, ffn2_norm, ffn2_w_in, ffn2_w_down, final_norm) = params
    depth, d = mix_norm.shape
    fw = w_fourier_out.shape[1]
    hw = w_hgrn_out.shape[1]
    heads = hw // HEAD_DIM
    d_ff = ffn1_w_down.shape[1]
    f_pad = -(-d_ff // ffn_tf) * ffn_tf
    cols = tuple(j * hw for j in (0, 1, 2, 3, 4))

    def ffn_weights(w_in_l, w_down_l):
        gate, up = w_in_l[:, :d_ff], w_in_l[:, d_ff:]
        w_gu = jnp.concatenate([_pad_cols(gate, f_pad), _pad_cols(up, f_pad)], axis=1).astype(BF16)
        w_dn = jnp.pad(w_down_l, ((0, f_pad - d_ff), (0, 0))).astype(BF16)
        return w_gu, w_dn

    lb = _lower_bound_call(lb_logits)
    dft = {seq: _dft_constants(seq, seq // DFT_INNER, DFT_INNER, FOURIER_GROUP_DIM, fw)
           for (_, _, seq) in groups}

    for l in range(depth):
        w_gu, w_dn = ffn_weights(ffn1_w_in[l], ffn1_w_down[l])
        x = _ffn_call(x, ffn1_norm[l][None, :], w_gu, w_dn, tm=ffn_tm, tf=ffn_tf)

        uf, mid, gates = _proj_call(x, mix_norm[l][None, :], w_in[l].astype(BF16), tm=proj_tm,
                                    tn=proj_tn, n_fourier=fw, n_gates=2 * d)
        fm_parts, ob_parts = [], []
        for (row0, batch, seq) in groups:
            fm_parts.append(_fourier_call(uf[row0:row0 + batch * seq], batch, seq, dft[seq],
                                          lane_chunk=dft_lanes))
            ob_parts.append(_scan_call(mid, lb[0, l][None, :], lb[1, l][None, :],
                                       hg_norm[l][None, :], row0=row0, batch=batch, seq=seq,
                                       heads=heads, cols=cols, tc=scan_tc,
                                       tc_bwd=min(scan_tc_bwd, seq)))
        x = _mixout_call(x, fm_parts, ob_parts, gates, w_fourier_out[l].astype(BF16),
                         w_hgrn_out[l].astype(BF16), w_out[l].astype(BF16), tm=out_tm)

        w_gu, w_dn = ffn_weights(ffn2_w_in[l], ffn2_w_down[l])
        x = _ffn_call(x, ffn2_norm[l][None, :], w_gu, w_dn, tm=ffn_tm, tf=ffn_tf)

    return _norm_call(x, final_norm[None, :], tm=ffn_tm)


def kernel(x_prompt, x_sample, ffn1_norm, ffn1_w_in, ffn1_w_down, mix_norm, w_in, lb_logits, hg_norm, w_fourier_out, w_hgrn_out, w_out, ffn2_norm, ffn2_w_in, ffn2_w_down, final_norm):
    bp, sp, d = x_prompt.shape
    bs, ss, _ = x_sample.shape
    x = jnp.concatenate([x_prompt.reshape(bp * sp, d), x_sample.reshape(bs * ss, d)], axis=0)
    groups = [(0, bp, sp), (bp * sp, bs, ss)]
    params = (ffn1_norm, ffn1_w_in, ffn1_w_down, mix_norm, w_in, lb_logits, hg_norm,
              w_fourier_out, w_hgrn_out, w_out, ffn2_norm, ffn2_w_in, ffn2_w_down, final_norm)
    y = _trunk(x, groups, params, ffn_tm=768, ffn_tf=512, proj_tm=1024, proj_tn=512,
               out_tm=256, scan_tc=1024, scan_tc_bwd=2048, dft_lanes=8192)
    return (y[:bp * sp].reshape(bp, sp, d), y[bp * sp:].reshape(bs, ss, d))
```

```python
import functools

import ml_dtypes
import numpy as np
import jax
import jax.numpy as jnp
from jax import lax
from jax.experimental import pallas as pl
from jax.experimental.pallas import tpu as pltpu

F32 = jnp.float32
BF16 = jnp.bfloat16

NORM_EPS = 1e-6
MIN_FORGET = 1e-30
LANES = 128
SUBLANES = 8
HEAD_DIM = 128
FOURIER_GROUP_DIM = 128
SCAN_CHUNK = 128
DFT_INNER = 128
VMEM_LIMIT = 56 * 1024 * 1024


def _cparams(sem):
    return pltpu.CompilerParams(dimension_semantics=sem, vmem_limit_bytes=VMEM_LIMIT)


def _rms_scale(x, gain):
    ms = jnp.mean(x * x, axis=-1, keepdims=True)
    return x * lax.rsqrt(ms + NORM_EPS) * gain


def _ffn_kernel(x_ref, gain_ref, wg_ref, wu_ref, wd_ref, o_ref, h_ref, *, last_cols):
    j = pl.program_id(1)
    last = pl.num_programs(1) - 1
    tf = wg_ref.shape[1]

    @pl.when(j == 0)
    def _():
        x = x_ref[...]
        h_ref[...] = _rms_scale(x, gain_ref[...]).astype(BF16)
        o_ref[...] = x

    def step(cols):
        h = h_ref[...]
        g = jnp.dot(h, wg_ref[:, :cols], preferred_element_type=F32)
        u = jnp.dot(h, wu_ref[:, :cols], preferred_element_type=F32)
        a = (g * jax.nn.sigmoid(g) * u * 0.5).astype(BF16)
        o_ref[...] += jnp.dot(a, wd_ref[:cols, :], preferred_element_type=F32)

    if last_cols == tf:
        step(tf)
    else:
        pl.when(j < last)(lambda: step(tf))
        pl.when(j == last)(lambda: step(last_cols))


def _ffn_call(x, gain, w_gate, w_up, w_down, *, tm, tf):
    t, d = x.shape
    f = w_down.shape[0]
    nf = pl.cdiv(f, tf)
    last_cols = f - (nf - 1) * tf
    assert last_cols % LANES == 0
    return pl.pallas_call(
        functools.partial(_ffn_kernel, last_cols=last_cols),
        grid=(t // tm, nf),
        in_specs=[
            pl.BlockSpec((tm, d), lambda i, j: (i, 0)),
            pl.BlockSpec((1, d), lambda i, j: (0, 0)),
            pl.BlockSpec((d, tf), lambda i, j: (0, j)),
            pl.BlockSpec((d, tf), lambda i, j: (0, j)),
            pl.BlockSpec((tf, d), lambda i, j: (j, 0)),
        ],
        out_specs=pl.BlockSpec((tm, d), lambda i, j: (i, 0)),
        out_shape=jax.ShapeDtypeStruct((t, d), F32),
        scratch_shapes=[pltpu.VMEM((tm, d), BF16)],
        compiler_params=_cparams(("parallel", "arbitrary")),
        name="ffn_half",
    )(x, gain, w_gate, w_up, w_down)


def _proj_kernel(x_ref, gain_ref, w_ref, uf_ref, mid_ref, gate_ref, h_ref, *, j_mid, j_gate):
    j = pl.program_id(1)

    @pl.when(j == 0)
    def _():
        h_ref[...] = _rms_scale(x_ref[...], gain_ref[...]).astype(BF16)

    def tile():
        return jnp.dot(h_ref[...], w_ref[...], preferred_element_type=F32)

    @pl.when(j < j_mid)
    def _():
        uf_ref[...] = tile()

    @pl.when((j >= j_mid) & (j < j_gate))
    def _():
        mid_ref[...] = tile()

    @pl.when(j >= j_gate)
    def _():
        gate_ref[...] = tile()


def _proj_call(x, gain, w, *, tm, tn, n_fourier, n_gates):
    t, d = x.shape
    n = w.shape[1]
    n_mid = n - n_fourier - n_gates
    j_mid = n_fourier // tn
    j_gate = (n_fourier + n_mid) // tn
    assert j_mid == 1 and n_fourier == tn and n_mid % tn == 0 and n_gates % tn == 0
    return pl.pallas_call(
        functools.partial(_proj_kernel, j_mid=j_mid, j_gate=j_gate),
        grid=(t // tm, n // tn),
        in_specs=[
            pl.BlockSpec((tm, d), lambda i, j: (i, 0)),
            pl.BlockSpec((1, d), lambda i, j: (0, 0)),
            pl.BlockSpec((d, tn), lambda i, j: (0, j)),
        ],
        out_specs=[
            pl.BlockSpec((tm, tn), lambda i, j: (i, 0)),
            pl.BlockSpec((tm, tn), lambda i, j: (i, jnp.clip(j - j_mid, 0, j_gate - j_mid - 1))),
            pl.BlockSpec((tm, tn), lambda i, j: (i, jnp.maximum(j - j_gate, 0))),
        ],
        out_shape=[jax.ShapeDtypeStruct((t, n_fourier), F32),
                   jax.ShapeDtypeStruct((t, n_mid), F32),
                   jax.ShapeDtypeStruct((t, n_gates), F32)],
        scratch_shapes=[pltpu.VMEM((tm, d), BF16)],
        compiler_params=_cparams(("parallel", "arbitrary")),
        name="mixer_proj",
    )(x, gain, w)


def _sigmoid(x):
    return 0.5 + 0.5 * jnp.tanh(0.5 * x)


def _mixout_kernel(*refs, tile_starts):
    ng = len(tile_starts)
    x_ref = refs[0]
    fm_refs = refs[1:1 + ng]
    ob_refs = refs[1 + ng:1 + 2 * ng]
    ga_ref, gb_ref, wfo_ref, who_ref, wo_ref, o_ref = refs[1 + 2 * ng:]
    i = pl.program_id(0)
    fm, ob = fm_refs[0][...], ob_refs[0][...]
    for g in range(1, ng):
        in_g = i >= tile_starts[g]
        fm = jnp.where(in_g, fm_refs[g][...], fm)
        ob = jnp.where(in_g, ob_refs[g][...], ob)
    a = jnp.dot(fm.astype(BF16), wfo_ref[...], preferred_element_type=F32)
    b = jnp.dot(ob, who_ref[...], preferred_element_type=F32)
    m = (_sigmoid(ga_ref[...]) * a + _sigmoid(gb_ref[...]) * b).astype(BF16)
    o_ref[...] = x_ref[...] + jnp.dot(m, wo_ref[...], preferred_element_type=F32)


def _mixout_call(x, fm_parts, ob_parts, gates, w_fo, w_ho, w_out, *, tm):
    t, d = x.shape
    tile_starts, tile_counts, start = [], [], 0
    for part in fm_parts:
        tile_starts.append(start)
        tile_counts.append(part.shape[0] // tm)
        start += part.shape[0] // tm

    def group_spec(part, g):
        return pl.BlockSpec(
            (tm, part.shape[1]),
            lambda i: (jnp.clip(i - tile_starts[g], 0, tile_counts[g] - 1), 0))

    def resident(w):
        return pl.BlockSpec(w.shape, lambda i: (0, 0), pipeline_mode=pl.Buffered(1))

    return pl.pallas_call(
        functools.partial(_mixout_kernel, tile_starts=tuple(tile_starts)),
        grid=(t // tm,),
        in_specs=(
            [pl.BlockSpec((tm, d), lambda i: (i, 0))]
            + [group_spec(p, g) for g, p in enumerate(fm_parts)]
            + [group_spec(p, g) for g, p in enumerate(ob_parts)]
            + [pl.BlockSpec((tm, d), lambda i: (i, 0)), pl.BlockSpec((tm, d), lambda i: (i, 1)),
               resident(w_fo), resident(w_ho), resident(w_out)]),
        out_specs=pl.BlockSpec((tm, d), lambda i: (i, 0)),
        out_shape=jax.ShapeDtypeStruct((t, d), F32),
        compiler_params=_cparams(("parallel",)),
        name="mixer_out",
    )(x, *fm_parts, *ob_parts, gates, gates, w_fo, w_ho, w_out)


def _norm_kernel(x_ref, gain_ref, o_ref):
    o_ref[...] = _rms_scale(x_ref[...], gain_ref[...])


def _norm_call(x, gain, *, tm):
    t, d = x.shape
    return pl.pallas_call(
        _norm_kernel,
        grid=(t // tm,),
        in_specs=[pl.BlockSpec((tm, d), lambda i: (i, 0)), pl.BlockSpec((1, d), lambda i: (0, 0))],
        out_specs=pl.BlockSpec((tm, d), lambda i: (i, 0)),
        out_shape=jax.ShapeDtypeStruct((t, d), F32),
        compiler_params=_cparams(("parallel",)),
        name="final_norm",
    )(x, gain)


def _lower_bound_kernel(logit_ref, o_ref, *, depth):
    for d0 in range(0, logit_ref.shape[0], depth):
        rows = [logit_ref[d0 + l:d0 + l + 1, :] for l in range(depth)]
        mx = functools.reduce(jnp.maximum, rows)
        es = [jnp.exp(r - mx) for r in rows]
        tot = functools.reduce(lambda a, b: a + b, es)
        ps = [e / tot for e in es]
        acc = ps[0]
        for l in range(depth):
            if l:
                acc = acc + ps[l]
            o_ref[d0 + l:d0 + l + 1, :] = jnp.maximum(acc - ps[0], 0.0)


def _lower_bound_call(lb_logits):
    dirs, depth, hw = lb_logits.shape
    lb = pl.pallas_call(
        functools.partial(_lower_bound_kernel, depth=depth),
        out_shape=jax.ShapeDtypeStruct((dirs * depth, hw), F32),
        name="lower_bounds",
    )(lb_logits.astype(F32).reshape(dirs * depth, hw))
    return lb.reshape(dirs, depth, hw)


def _split2_np(m):
    hi = m.astype(ml_dtypes.bfloat16).astype(np.float64)
    lo = m - hi
    return hi, lo


def _pad_to(m, axis, mult):
    pad = -m.shape[axis] % mult
    widths = [(0, 0)] * m.ndim
    widths[axis] = (0, pad)
    return np.pad(m, widths)


def _split2(x):
    hi = x.astype(BF16)
    lo = (x - hi.astype(F32)).astype(BF16)
    return hi, lo


def _dft_constants(s, n1, n2, gd, lanes):
    k1 = np.arange(n1, dtype=np.float64)
    c1 = np.cos(2 * np.pi * np.outer(k1, k1) / n1)
    s1 = np.sin(2 * np.pi * np.outer(k1, k1) / n1)
    m1 = np.concatenate([c1, -s1], axis=0)
    m1h, m1l = _split2_np(m1)
    m1cat = _pad_to(np.concatenate([m1h, m1h, m1l], axis=1), 1, LANES)

    k2 = np.arange(n2, dtype=np.float64)
    c2 = np.cos(2 * np.pi * np.outer(k2, k2) / n2)
    s2 = np.sin(2 * np.pi * np.outer(k2, k2) / n2)
    m2 = np.block([[c2, s2], [-s2, c2]])
    m2h, m2l = _split2_np(m2)
    m2cat = np.concatenate([m2h, m2h, m2l], axis=1)

    kc = np.arange(gd, dtype=np.float64)
    scale = 1.0 / np.sqrt(float(s) * gd)
    cc = np.cos(2 * np.pi * np.outer(kc, kc) / gd) * scale
    sc = np.sin(2 * np.pi * np.outer(kc, kc) / gd) * scale
    mc = np.concatenate([cc, sc], axis=0)
    mch, mcl = _split2_np(mc)
    mccat = np.concatenate([mch, mch, mcl], axis=0)

    ang = 2 * np.pi * np.outer(k1, np.arange(n2, dtype=np.float64)) / s
    twc = jnp.repeat(jnp.asarray(np.cos(ang), F32), lanes, axis=1)
    tws = jnp.repeat(jnp.asarray(np.sin(ang), F32), lanes, axis=1)
    return (jnp.asarray(m1cat, BF16), jnp.asarray(m2cat, BF16), jnp.asarray(mccat, BF16), twc, tws)


def _dft1_kernel(x_ref, m1_ref, twc_ref, tws_ref, o_ref):
    n1 = x_ref.shape[1]
    xh, xl = _split2(x_ref[0])
    pieces = [xh, xl, xh]
    k_pad = m1_ref.shape[1] - 3 * n1
    if k_pad:
        pieces.append(jnp.zeros((k_pad, xh.shape[1]), BF16))
    t = jnp.dot(m1_ref[...], jnp.concatenate(pieces, axis=0), preferred_element_type=F32)
    tr, ti = t[:n1], t[n1:]
    c, s = twc_ref[...], tws_ref[...]
    o_ref[0, 0] = tr * c + ti * s
    o_ref[0, 1] = ti * c - tr * s


def _dft2_kernel(t_ref, m2_ref, mc_ref, o_ref):
    n2 = t_ref.shape[3]
    gd = mc_ref.shape[1]
    t = jnp.concatenate([t_ref[0, 0, 0], t_ref[0, 1, 0]], axis=0)
    th, tl = _split2(t)
    g = jnp.dot(m2_ref[...], jnp.concatenate([th, tl, th], axis=0), preferred_element_type=F32)
    gr, gi = g[:n2], g[n2:]
    for grp in range(t.shape[1] // gd):
        sl = slice(grp * gd, (grp + 1) * gd)
        gg = jnp.concatenate([gr[:, sl], gi[:, sl]], axis=1)
        gh, gl = _split2(gg)
        o_ref[0, :, sl] = jnp.dot(jnp.concatenate([gh, gl, gh], axis=1), mc_ref[...],
                                  preferred_element_type=F32)


def _fourier_call(uf, batch, seq, consts, *, lane_chunk):
    w = uf.shape[1]
    n2 = DFT_INNER
    n1 = seq // n2
    m1cat, m2cat, mccat, twc, tws = consts
    x = uf.reshape(batch, n1, n2 * w)
    tw = pl.pallas_call(
        _dft1_kernel,
        grid=(batch, n2 * w // lane_chunk),
        in_specs=[
            pl.BlockSpec((1, n1, lane_chunk), lambda b, j: (b, 0, j)),
            pl.BlockSpec(m1cat.shape, lambda b, j: (0, 0)),
            pl.BlockSpec((n1, lane_chunk), lambda b, j: (0, j)),
            pl.BlockSpec((n1, lane_chunk), lambda b, j: (0, j)),
        ],
        out_specs=pl.BlockSpec((1, 2, n1, lane_chunk), lambda b, j: (b, 0, 0, j)),
        out_shape=jax.ShapeDtypeStruct((batch, 2, n1, n2 * w), F32),
        compiler_params=_cparams(("parallel", "parallel")),
        name="dft_stage1",
    )(x, m1cat, twc, tws)
    tw = tw.reshape(batch, 2, n1, n2, w)
    y = pl.pallas_call(
        _dft2_kernel,
        grid=(batch, n1),
        in_specs=[
            pl.BlockSpec((1, 2, 1, n2, w), lambda b, k: (b, 0, k, 0, 0)),
            pl.BlockSpec(m2cat.shape, lambda b, k: (0, 0)),
            pl.BlockSpec(mccat.shape, lambda b, k: (0, 0)),
        ],
        out_specs=pl.BlockSpec((1, n2, w), lambda b, k: (b, 0, k)),
        out_shape=jax.ShapeDtypeStruct((batch, n2, n1 * w), F32),
        compiler_params=_cparams(("parallel", "parallel")),
        name="dft_stage2",
    )(tw, m2cat, mccat)
    return y.reshape(batch * seq, w)


def _silu(x):
    h = 0.5 * x
    return h + h * jnp.tanh(h)


def _gate_consts(lb):
    c1 = 0.5 * (1.0 - lb)
    return lb + c1, c1


def _gates(z, c0, c1):
    f = c0 + c1 * jnp.tanh(0.5 * z)
    f_floor = jnp.maximum(f, MIN_FORGET)
    return f_floor, jnp.log(f_floor), 1.0 - f


def _cumsum_rows(x, tri):
    w = x.shape[1]
    hi = x.astype(BF16)
    lo = (x - hi.astype(F32)).astype(BF16)
    y = jnp.dot(tri, jnp.concatenate([hi, lo], axis=1), preferred_element_type=F32)
    return y[:, :w] + y[:, w:]


def _boundary_rows(x, n, row8):
    c, w = x.shape
    parts = []
    if 2 * n >= SUBLANES:
        for p in range(c // (2 * n)):
            r = p * 2 * n + n - 1
            parts.append(jnp.broadcast_to(x[r:r + 1, :], (2 * n, w)))
    else:
        lo = row8 < 4
        for p in range(c // SUBLANES):
            r = p * SUBLANES
            a = jnp.broadcast_to(x[r + 1:r + 2, :], (SUBLANES, w))
            b = jnp.broadcast_to(x[r + 5:r + 6, :], (SUBLANES, w))
            parts.append(jnp.where(lo, a, b))
    return jnp.concatenate(parts, axis=0)


def _nt_dot(a, b):
    return lax.dot_general(a, b, (((1,), (1,)), ((), ())), preferred_element_type=F32)


def _intra_scores(q, kf, kb, ff, fb, b_f, bx_b, b_b, row, row8, xor):
    c = q.shape[0]
    products = []
    n = c // 2
    while n >= 1:
        odd = (row & n) != 0
        kk = jnp.where(odd, kb, kf)
        if n == 1:
            qt = q * jnp.where(odd, ff, fb)
            kt = kk
        else:
            d1 = b_f - _boundary_rows(b_f, n, row8)
            d2 = _boundary_rows(b_b, n, row8) - bx_b
            qt = q * jnp.exp(jnp.minimum(d1, d2))
            kt = kk * jnp.exp(-jnp.maximum(d1, d2))
        products.append((n, _nt_dot(qt.astype(BF16), kt.astype(BF16))))
        n //= 2
    return products


def _assemble_scores(products, xor):
    p = None
    for n, pn in products:
        p = pn if p is None else jnp.where(xor < 2 * n, pn, p)
    return jnp.where(xor == 0, 0.0, p)


def _scan_fwd_kernel(q_ref, zf_ref, zb_ref, v_ref, lbf_ref, lbb_ref,
                     o_ref, qb_ref, kb_ref, tot_ref, vt_ref, st_ref, *, chunk):
    @pl.when(pl.program_id(2) == 0)
    def _():
        st_ref[...] = jnp.zeros_like(st_ref)

    c = chunk
    w = q_ref.shape[1]
    ri = lax.broadcasted_iota(jnp.int32, (c, c), 0)
    ci = lax.broadcasted_iota(jnp.int32, (c, c), 1)
    tri = (ri >= ci).astype(BF16)
    xor = ri ^ ci
    row = lax.broadcasted_iota(jnp.int32, (c, w), 0)
    row8 = lax.broadcasted_iota(jnp.int32, (SUBLANES, w), 0)
    c0f, c1f = _gate_consts(lbf_ref[...])
    c0b, c1b = _gate_consts(lbb_ref[...])

    nch = q_ref.shape[0] // c
    work = [dict(sl=pl.ds(ch * c, c), ch=ch) for ch in range(nch)]

    def stage_gates(s):
        sl = s["sl"]
        s["q"] = _silu(q_ref[sl, :])
        s["v"] = v_ref[sl, :]
        s["ff"], lf, s["kf"] = _gates(zf_ref[sl, :], c0f, c1f)
        s["fb"], s["lfb"], s["kb"] = _gates(zb_ref[sl, :], c0b, c1b)
        s["b_f"] = _cumsum_rows(lf, tri)
        s["b_b"] = _cumsum_rows(s["lfb"], tri)

    def stage_levels(s):
        q, v, kf, kb, b_f, b_b = s["q"], s["v"], s["kf"], s["kb"], s["b_f"], s["b_b"]
        bx_b = b_b - s["lfb"]
        s["products"] = _intra_scores(q, kf, kb, s["ff"], s["fb"], b_f, bx_b, b_b, row, row8, xor)
        s["diag"] = jnp.sum(q * (kf + kb), axis=-1, keepdims=True)
        b_last = b_f[c - 1:c, :]
        s["qh"] = (q * jnp.exp(b_f)).astype(BF16)
        kh = (kf * jnp.exp(b_last - b_f)).astype(BF16)
        vt = v.T.astype(BF16)
        vt_ref[:, s["sl"]] = vt
        s["upd"] = jnp.dot(vt, kh, preferred_element_type=F32)
        s["decay"] = jnp.exp(b_last)
        tot = b_b[c - 1:c, :]
        qb_ref[s["sl"], :] = (q * jnp.exp(tot - bx_b)).astype(BF16)
        kb_ref[s["sl"], :] = (kb * jnp.exp(bx_b)).astype(BF16)
        tot_ref[pl.ds(s["ch"] * SUBLANES, SUBLANES), :] = jnp.broadcast_to(jnp.exp(tot),
                                                                            (SUBLANES, w))

    def stage_scores(s):
        p = _assemble_scores(s.pop("products"), xor)
        s["pv"] = jnp.dot(p.astype(BF16), s["v"].astype(BF16), preferred_element_type=F32)

    stages = (stage_gates, stage_levels, stage_scores)
    for step in range(nch + len(stages) - 1):
        for k, stage in enumerate(stages):
            if 0 <= step - k < nch:
                stage(work[step - k])

    st = st_ref[...]
    for s in work:
        o_ref[s["sl"], :] = s["pv"] + s["diag"] * s["v"] + _nt_dot(s["qh"], st.astype(BF16))
        st = st * s["decay"] + s["upd"]
    st_ref[...] = st


def _scan_bwd_kernel(qb_ref, kb_ref, tot_ref, vt_ref, g_ref, o1_ref, hg_ref, o_ref, st_ref, *,
                     chunk, lookahead):
    @pl.when(pl.program_id(2) == 0)
    def _():
        st_ref[...] = jnp.zeros_like(st_ref)

    c = chunk
    nch = o1_ref.shape[0] // c
    hg = hg_ref[...]
    order = list(reversed(range(nch)))
    slices = {ch: pl.ds(ch * c, c) for ch in order}

    upd = {ch: jnp.dot(vt_ref[:, slices[ch]], kb_ref[slices[ch], :], preferred_element_type=F32)
           for ch in order}
    st = st_ref[...]
    st_in = {}
    for ch in order:
        st_in[ch] = st.astype(BF16)
        st = st * tot_ref[pl.ds(ch * SUBLANES, 1), :] + upd[ch]
    st_ref[...] = st

    inter = {}
    for i in range(nch + lookahead):
        if i < nch:
            ch = order[i]
            inter[ch] = _nt_dot(qb_ref[slices[ch], :], st_in.pop(ch))
        if i >= lookahead:
            ch = order[i - lookahead]
            o = o1_ref[slices[ch], :] + inter.pop(ch)
            o = o * lax.rsqrt(jnp.mean(o * o, axis=-1, keepdims=True) + NORM_EPS)
            o_ref[slices[ch], :] = (o * hg * _silu(g_ref[slices[ch], :])).astype(BF16)


def _scan_call(mid, lb_f, lb_b, hg_gain, *, row0, batch, seq, heads, cols, tc, tc_bwd):
    hd = HEAD_DIM
    nb = seq // tc
    rb0 = row0 // tc
    cq, czf, czb, cv, cg = (c // hd for c in cols)
    rows_per_tot = SCAN_CHUNK // SUBLANES
    hw = heads * hd

    def tok(col):
        return pl.BlockSpec((tc, hd), lambda b, h, j: (rb0 + b * nb + j, col + h))

    def own(rows):
        return pl.BlockSpec((rows, hd), lambda b, h, j: (b * nb + j, h))

    par = pl.BlockSpec((1, hd), lambda b, h, j: (0, h))
    o1, qb, kb, tot, vt = pl.pallas_call(
        functools.partial(_scan_fwd_kernel, chunk=SCAN_CHUNK),
        grid=(batch, heads, nb),
        in_specs=[tok(cq), tok(czf), tok(czb), tok(cv), par, par],
        out_specs=[own(tc), own(tc), own(tc), own(tc // rows_per_tot),
                   pl.BlockSpec((hd, tc), lambda b, h, j: (h, b * nb + j))],
        out_shape=[jax.ShapeDtypeStruct((batch * seq, hw), F32),
                   jax.ShapeDtypeStruct((batch * seq, hw), BF16),
                   jax.ShapeDtypeStruct((batch * seq, hw), BF16),
                   jax.ShapeDtypeStruct((batch * seq // rows_per_tot, hw), F32),
                   jax.ShapeDtypeStruct((hw, batch * seq), BF16)],
        scratch_shapes=[pltpu.VMEM((hd, hd), F32)],
        compiler_params=_cparams(("parallel", "parallel", "arbitrary")),
        name="hgrn2_fwd",
    )(mid, mid, mid, mid, lb_f, lb_b)

    nb2 = seq // tc_bwd
    rb2 = row0 // tc_bwd

    def own_rev(rows):
        return pl.BlockSpec((rows, hd), lambda b, h, j: (b * nb2 + nb2 - 1 - j, h))

    def tok_rev(col):
        return pl.BlockSpec((tc_bwd, hd), lambda b, h, j: (rb2 + b * nb2 + nb2 - 1 - j, col + h))

    return pl.pallas_call(
        functools.partial(_scan_bwd_kernel, chunk=SCAN_CHUNK, lookahead=2),
        grid=(batch, heads, nb2),
        in_specs=[own_rev(tc_bwd), own_rev(tc_bwd), own_rev(tc_bwd // rows_per_tot),
                  pl.BlockSpec((hd, tc_bwd), lambda b, h, j: (h, b * nb2 + nb2 - 1 - j)),
                  tok_rev(cg), own_rev(tc_bwd), par],
        out_specs=own_rev(tc_bwd),
        out_shape=jax.ShapeDtypeStruct((batch * seq, hw), BF16),
        scratch_shapes=[pltpu.VMEM((hd, hd), F32)],
        compiler_params=_cparams(("parallel", "parallel", "arbitrary")),
        name="hgrn2_bwd",
    )(qb, kb, tot, vt, mid, o1, hg_gain)


def _pad_cols(w, n):
    return jnp.pad(w, ((0, 0), (0, n - w.shape[1])))


def _trunk(x, groups, params, *, ffn_tm, ffn_tf, proj_tm, proj_tn, out_tm, scan_tc, scan_tc_bwd,
           dft_lanes):
    (ffn1_norm, ffn1_w_in, ffn1_w_down, mix_norm, w_in, lb_logits, hg_norm, w_fourier_out,
     w_hgrn_out, w_out, ffn2_norm, ffn2_w_in, ffn2_w_down, final_norm) = params
    depth, d = mix_norm.shape
    fw = w_fourier_out.shape[1]
    hw = w_hgrn_out.shape[1]
    heads = hw // HEAD_DIM
    d_ff = ffn1_w_down.shape[1]
    cols = tuple(j * hw for j in (0, 1, 2, 3, 4))

    def ffn_half(x, norm, w_in_l, w_down_l):
        return _ffn_call(x, norm[None, :], w_in_l[:, :d_ff].astype(BF16),
                         w_in_l[:, d_ff:].astype(BF16), w_down_l.astype(BF16),
                         tm=ffn_tm, tf=ffn_tf)

    lb = _lower_bound_call(lb_logits)
    dft = {seq: _dft_constants(seq, seq // DFT_INNER, DFT_INNER, FOURIER_GROUP_DIM, fw)
           for (_, _, seq) in groups}

    for l in range(depth):
        x = ffn_half(x, ffn1_norm[l], ffn1_w_in[l], ffn1_w_down[l])

        uf, mid, gates = _proj_call(x, mix_norm[l][None, :], w_in[l].astype(BF16), tm=proj_tm,
                                    tn=proj_tn, n_fourier=fw, n_gates=2 * d)
        fm_parts, ob_parts = [], []
        for (row0, batch, seq) in groups:
            fm_parts.append(_fourier_call(uf[row0:row0 + batch * seq], batch, seq, dft[seq],
                                          lane_chunk=dft_lanes))
            ob_parts.append(_scan_call(mid, lb[0, l][None, :], lb[1, l][None, :],
                                       hg_norm[l][None, :], row0=row0, batch=batch, seq=seq,
                                       heads=heads, cols=cols, tc=scan_tc,
                                       tc_bwd=min(scan_tc_bwd, seq)))
        x = _mixout_call(x, fm_parts, ob_parts, gates, w_fourier_out[l].astype(BF16),
                         w_hgrn_out[l].astype(BF16), w_out[l].astype(BF16), tm=out_tm)

        x = ffn_half(x, ffn2_norm[l], ffn2_w_in[l], ffn2_w_down[l])

    return _norm_call(x, final_norm[None, :], tm=ffn_tm)


def kernel(x_prompt, x_sample, ffn1_norm, ffn1_w_in, ffn1_w_down, mix_norm, w_in, lb_logits, hg_norm, w_fourier_out, w_hgrn_out, w_out, ffn2_norm, ffn2_w_in, ffn2_w_down, final_norm):
    bp, sp, d = x_prompt.shape
    bs, ss, _ = x_sample.shape
    x = jnp.concatenate([x_prompt.reshape(bp * sp, d), x_sample.reshape(bs * ss, d)], axis=0)
    groups = [(0, bp, sp), (bp * sp, bs, ss)]
    params = (ffn1_norm, ffn1_w_in, ffn1_w_down, mix_norm, w_in, lb_logits, hg_norm,
              w_fourier_out, w_hgrn_out, w_out, ffn2_norm, ffn2_w_in, ffn2_w_down, final_norm)
    y = _trunk(x, groups, params, ffn_tm=768, ffn_tf=512, proj_tm=1024, proj_tn=512,
               out_tm=256, scan_tc=1024, scan_tc_bwd=2048, dft_lanes=8192)
    return (y[:bp * sp].reshape(bp, sp, d), y[bp * sp:].reshape(bs, ss, d))
```

```python
import functools

import ml_dtypes
import numpy as np
import jax
import jax.numpy as jnp
from jax import lax
from jax.experimental import pallas as pl
from jax.experimental.pallas import tpu as pltpu

F32 = jnp.float32
BF16 = jnp.bfloat16

NORM_EPS = 1e-6
MIN_FORGET = 1e-30
LANES = 128
SUBLANES = 8
HEAD_DIM = 128
FOURIER_GROUP_DIM = 128
SCAN_CHUNK = 128
DFT_BLOCK = 128
VMEM_LIMIT = 56 * 1024 * 1024


def _cparams(sem):
    return pltpu.CompilerParams(dimension_semantics=sem, vmem_limit_bytes=VMEM_LIMIT)


def _rms_scale(x, gain):
    ms = jnp.mean(x * x, axis=-1, keepdims=True)
    return x * lax.rsqrt(ms + NORM_EPS) * gain


def _ffn_kernel(x_ref, gain_ref, wg_ref, wu_ref, wd_ref, o_ref, h_ref, *, last_cols):
    j = pl.program_id(1)
    last = pl.num_programs(1) - 1
    tf = wg_ref.shape[1]

    @pl.when(j == 0)
    def _():
        x = x_ref[...]
        h_ref[...] = _rms_scale(x, gain_ref[...]).astype(BF16)
        o_ref[...] = x

    def step(cols):
        h = h_ref[...]
        g = jnp.dot(h, wg_ref[:, :cols], preferred_element_type=F32)
        u = jnp.dot(h, wu_ref[:, :cols], preferred_element_type=F32)
        a = (g * jax.nn.sigmoid(g) * u * 0.5).astype(BF16)
        o_ref[...] += jnp.dot(a, wd_ref[:cols, :], preferred_element_type=F32)

    if last_cols == tf:
        step(tf)
    else:
        pl.when(j < last)(lambda: step(tf))
        pl.when(j == last)(lambda: step(last_cols))


def _ffn_call(x, gain, w_gate, w_up, w_down, *, tm, tf):
    t, d = x.shape
    f = w_down.shape[0]
    nf = pl.cdiv(f, tf)
    last_cols = f - (nf - 1) * tf
    assert last_cols % LANES == 0
    return pl.pallas_call(
        functools.partial(_ffn_kernel, last_cols=last_cols),
        grid=(t // tm, nf),
        in_specs=[
            pl.BlockSpec((tm, d), lambda i, j: (i, 0)),
            pl.BlockSpec((1, d), lambda i, j: (0, 0)),
            pl.BlockSpec((d, tf), lambda i, j: (0, j)),
            pl.BlockSpec((d, tf), lambda i, j: (0, j)),
            pl.BlockSpec((tf, d), lambda i, j: (j, 0)),
        ],
        out_specs=pl.BlockSpec((tm, d), lambda i, j: (i, 0)),
        out_shape=jax.ShapeDtypeStruct((t, d), F32),
        scratch_shapes=[pltpu.VMEM((tm, d), BF16)],
        compiler_params=_cparams(("parallel", "arbitrary")),
        name="ffn_half",
    )(x, gain, w_gate, w_up, w_down)


def _proj_kernel(x_ref, gain_ref, w_ref, act_ref, gate_ref, h_ref, *, j_gate):
    j = pl.program_id(1)

    @pl.when(j == 0)
    def _():
        h_ref[...] = _rms_scale(x_ref[...], gain_ref[...]).astype(BF16)

    def tile():
        return jnp.dot(h_ref[...], w_ref[...], preferred_element_type=F32)

    @pl.when(j < j_gate)
    def _():
        act_ref[...] = tile()

    @pl.when(j >= j_gate)
    def _():
        gate_ref[...] = tile()


def _proj_call(x, gain, w, *, tm, tn, n_gates):
    t, d = x.shape
    n = w.shape[1]
    n_act = n - n_gates
    j_gate = n_act // tn
    assert n_act % tn == 0 and n_gates % tn == 0
    return pl.pallas_call(
        functools.partial(_proj_kernel, j_gate=j_gate),
        grid=(t // tm, n // tn),
        in_specs=[
            pl.BlockSpec((tm, d), lambda i, j: (i, 0)),
            pl.BlockSpec((1, d), lambda i, j: (0, 0)),
            pl.BlockSpec((d, tn), lambda i, j: (0, j)),
        ],
        out_specs=[
            pl.BlockSpec((tm, tn), lambda i, j: (i, jnp.minimum(j, j_gate - 1))),
            pl.BlockSpec((tm, tn), lambda i, j: (i, jnp.maximum(j - j_gate, 0))),
        ],
        out_shape=[jax.ShapeDtypeStruct((t, n_act), F32),
                   jax.ShapeDtypeStruct((t, n_gates), F32)],
        scratch_shapes=[pltpu.VMEM((tm, d), BF16)],
        compiler_params=_cparams(("parallel", "arbitrary")),
        name="mixer_proj",
    )(x, gain, w)


def _sigmoid(x):
    return 0.5 + 0.5 * jnp.tanh(0.5 * x)


def _mixout_kernel(*refs, tile_starts):
    ng = len(tile_starts)
    x_ref = refs[0]
    fm_refs = refs[1:1 + ng]
    ob_refs = refs[1 + ng:1 + 2 * ng]
    ga_ref, gb_ref, wfo_ref, who_ref, wo_ref, o_ref = refs[1 + 2 * ng:]
    i = pl.program_id(0)
    fm, ob = fm_refs[0][...], ob_refs[0][...]
    for g in range(1, ng):
        in_g = i >= tile_starts[g]
        fm = jnp.where(in_g, fm_refs[g][...], fm)
        ob = jnp.where(in_g, ob_refs[g][...], ob)
    a = jnp.dot(fm.astype(BF16), wfo_ref[...], preferred_element_type=F32)
    b = jnp.dot(ob, who_ref[...], preferred_element_type=F32)
    m = (_sigmoid(ga_ref[...]) * a + _sigmoid(gb_ref[...]) * b).astype(BF16)
    o_ref[...] = x_ref[...] + jnp.dot(m, wo_ref[...], preferred_element_type=F32)


def _mixout_call(x, fm_parts, ob_parts, gates, w_fo, w_ho, w_out, *, tm):
    t, d = x.shape
    tile_starts, tile_counts, start = [], [], 0
    for part in fm_parts:
        tile_starts.append(start)
        tile_counts.append(part.shape[0] // tm)
        start += part.shape[0] // tm

    def group_spec(part, g):
        return pl.BlockSpec(
            (tm, part.shape[1]),
            lambda i: (jnp.clip(i - tile_starts[g], 0, tile_counts[g] - 1), 0))

    def resident(w):
        return pl.BlockSpec(w.shape, lambda i: (0, 0), pipeline_mode=pl.Buffered(1))

    return pl.pallas_call(
        functools.partial(_mixout_kernel, tile_starts=tuple(tile_starts)),
        grid=(t // tm,),
        in_specs=(
            [pl.BlockSpec((tm, d), lambda i: (i, 0))]
            + [group_spec(p, g) for g, p in enumerate(fm_parts)]
            + [group_spec(p, g) for g, p in enumerate(ob_parts)]
            + [pl.BlockSpec((tm, d), lambda i: (i, 0)), pl.BlockSpec((tm, d), lambda i: (i, 1)),
               resident(w_fo), resident(w_ho), resident(w_out)]),
        out_specs=pl.BlockSpec((tm, d), lambda i: (i, 0)),
        out_shape=jax.ShapeDtypeStruct((t, d), F32),
        compiler_params=_cparams(("parallel",)),
        name="mixer_out",
    )(x, *fm_parts, *ob_parts, gates, gates, w_fo, w_ho, w_out)


def _norm_kernel(x_ref, gain_ref, o_ref):
    o_ref[...] = _rms_scale(x_ref[...], gain_ref[...])


def _norm_call(x, gain, *, tm):
    t, d = x.shape
    return pl.pallas_call(
        _norm_kernel,
        grid=(t // tm,),
        in_specs=[pl.BlockSpec((tm, d), lambda i: (i, 0)), pl.BlockSpec((1, d), lambda i: (0, 0))],
        out_specs=pl.BlockSpec((tm, d), lambda i: (i, 0)),
        out_shape=jax.ShapeDtypeStruct((t, d), F32),
        compiler_params=_cparams(("parallel",)),
        name="final_norm",
    )(x, gain)


def _lower_bound_kernel(logit_ref, o_ref, *, depth):
    for d0 in range(0, logit_ref.shape[0], depth):
        rows = [logit_ref[d0 + l:d0 + l + 1, :] for l in range(depth)]
        mx = functools.reduce(jnp.maximum, rows)
        es = [jnp.exp(r - mx) for r in rows]
        tot = functools.reduce(lambda a, b: a + b, es)
        ps = [e / tot for e in es]
        acc = ps[0]
        for l in range(depth):
            if l:
                acc = acc + ps[l]
            o_ref[d0 + l:d0 + l + 1, :] = jnp.maximum(acc - ps[0], 0.0)


def _lower_bound_call(lb_logits):
    dirs, depth, hw = lb_logits.shape
    lb = pl.pallas_call(
        functools.partial(_lower_bound_kernel, depth=depth),
        out_shape=jax.ShapeDtypeStruct((dirs * depth, hw), F32),
        name="lower_bounds",
    )(lb_logits.astype(F32).reshape(dirs * depth, hw))
    return lb.reshape(dirs, depth, hw)


def _split2_np(m):
    hi = m.astype(ml_dtypes.bfloat16).astype(np.float64)
    lo = m - hi
    return hi, lo


def _split2(x):
    hi = x.astype(BF16)
    lo = (x - hi.astype(F32)).astype(BF16)
    return hi, lo


def _dft_constants(s, gd):
    nr = DFT_BLOCK
    kr = np.arange(nr, dtype=np.float64)
    cr = np.cos(2 * np.pi * np.outer(kr, kr) / nr)
    sr = np.sin(2 * np.pi * np.outer(kr, kr) / nr)
    mr = np.block([[cr, sr], [-sr, cr]])
    mrh, mrl = _split2_np(mr)
    mrcat = np.concatenate([mrh, mrh, mrl], axis=1)

    kc = np.arange(gd, dtype=np.float64)
    scale = 1.0 / np.sqrt(float(s) * gd)
    cc = np.cos(2 * np.pi * np.outer(kc, kc) / gd) * scale
    sc = np.sin(2 * np.pi * np.outer(kc, kc) / gd) * scale
    mc = np.concatenate([cc, sc], axis=0)
    mch, mcl = _split2_np(mc)
    mccat = np.concatenate([mch, mch, mcl], axis=0)

    k2 = np.arange(s // nr, dtype=np.float64)
    ang = (2 * np.pi * np.outer(k2, kr) / s).reshape(s, 1)
    twc = jnp.broadcast_to(jnp.asarray(np.cos(ang), F32), (s, LANES))
    tws = jnp.broadcast_to(jnp.asarray(np.sin(ang), F32), (s, LANES))
    return jnp.asarray(mrcat, BF16), jnp.asarray(mccat, BF16), twc, tws


def _c_add(a, b):
    return b if a is None else a if b is None else a + b


def _c_sub(a, b):
    return a if b is None else -b if a is None else a - b


def _fft_blocks(xs):
    n = len(xs)
    if n == 1:
        return xs
    even, odd = _fft_blocks(xs[0::2]), _fft_blocks(xs[1::2])
    out = [None] * n
    for k in range(n // 2):
        o_r, o_i = odd[k]
        if k == 0:
            t_r, t_i = o_r, o_i
        elif 4 * k == n:
            t_r, t_i = o_i, (None if o_r is None else -o_r)
        else:
            wr, wi = float(np.cos(2 * np.pi * k / n)), float(np.sin(2 * np.pi * k / n))
            t_r = _c_add(None if o_r is None else wr * o_r, None if o_i is None else wi * o_i)
            t_i = _c_sub(None if o_i is None else wr * o_i, None if o_r is None else wi * o_r)
        e_r, e_i = even[k]
        out[k] = (_c_add(e_r, t_r), _c_add(e_i, t_i))
        out[k + n // 2] = (_c_sub(e_r, t_r), _c_sub(e_i, t_i))
    return out


def _dft_kernel(x_ref, twc_ref, tws_ref, mr_ref, mc_ref, o_ref):
    nr = DFT_BLOCK
    nb = x_ref.shape[0] // nr

    def rows(ref, j):
        return ref[pl.ds(j * nr, nr), :]

    u = _fft_blocks([(rows(x_ref, j), None) for j in range(nb)])
    for k2 in range(nb):
        u_r, u_i = u[k2]
        u_r = jnp.zeros((nr, x_ref.shape[1]), F32) if u_r is None else u_r
        u_i = jnp.zeros_like(u_r) if u_i is None else u_i
        if k2 == 0:
            t_r, t_i = u_r, u_i
        else:
            c, s = rows(twc_ref, k2), rows(tws_ref, k2)
            t_r, t_i = u_r * c + u_i * s, u_i * c - u_r * s
        th, tl = _split2(jnp.concatenate([t_r, t_i], axis=0))
        v = jnp.dot(mr_ref[...], jnp.concatenate([th, tl, th], axis=0),
                    preferred_element_type=F32)
        vh, vl = _split2(jnp.concatenate([v[:nr], v[nr:]], axis=1))
        o_ref[0, k2] = jnp.dot(jnp.concatenate([vh, vl, vh], axis=1), mc_ref[...],
                               preferred_element_type=F32)


def _fourier_call(act, consts, *, row0, batch, seq, col0, width):
    gd = FOURIER_GROUP_DIM
    nb = seq // DFT_BLOCK
    mrcat, mccat, twc, tws = consts

    def resident(a):
        return pl.BlockSpec(a.shape, lambda b, g: (0, 0), pipeline_mode=pl.Buffered(1))

    y = pl.pallas_call(
        _dft_kernel,
        grid=(batch, width // gd),
        in_specs=[pl.BlockSpec((seq, gd), lambda b, g: (row0 // seq + b, col0 // gd + g)),
                  resident(twc), resident(tws), resident(mrcat), resident(mccat)],
        out_specs=pl.BlockSpec((1, nb, DFT_BLOCK, gd), lambda b, g: (b, 0, 0, g)),
        out_shape=jax.ShapeDtypeStruct((batch, nb, DFT_BLOCK, width), F32),
        compiler_params=_cparams(("parallel", "parallel")),
        name="dft2d",
    )(act, twc, tws, mrcat, mccat)
    return y.transpose(0, 2, 1, 3).reshape(batch * seq, width)


def _silu(x):
    h = 0.5 * x
    return h + h * jnp.tanh(h)


def _gate_consts(lb):
    c1 = 0.5 * (1.0 - lb)
    return lb + c1, c1


def _gates(z, c0, c1):
    f = c0 + c1 * jnp.tanh(0.5 * z)
    f_floor = jnp.maximum(f, MIN_FORGET)
    return f_floor, jnp.log(f_floor), 1.0 - f


def _cumsum_rows(x, tri):
    w = x.shape[1]
    hi = x.astype(BF16)
    lo = (x - hi.astype(F32)).astype(BF16)
    y = jnp.dot(tri, jnp.concatenate([hi, lo], axis=1), preferred_element_type=F32)
    return y[:, :w] + y[:, w:]


def _boundary_rows(x, n, row8):
    c, w = x.shape
    parts = []
    if 2 * n >= SUBLANES:
        for p in range(c // (2 * n)):
            r = p * 2 * n + n - 1
            parts.append(jnp.broadcast_to(x[r:r + 1, :], (2 * n, w)))
    else:
        lo = row8 < 4
        for p in range(c // SUBLANES):
            r = p * SUBLANES
            a = jnp.broadcast_to(x[r + 1:r + 2, :], (SUBLANES, w))
            b = jnp.broadcast_to(x[r + 5:r + 6, :], (SUBLANES, w))
            parts.append(jnp.where(lo, a, b))
    return jnp.concatenate(parts, axis=0)


def _nt_dot(a, b):
    return lax.dot_general(a, b, (((1,), (1,)), ((), ())), preferred_element_type=F32)


def _intra_scores(q, kf, kb, ff, fb, b_f, bx_b, b_b, row, row8, xor):
    c = q.shape[0]
    products = []
    n = c // 2
    while n >= 1:
        odd = (row & n) != 0
        kk = jnp.where(odd, kb, kf)
        if n == 1:
            qt = q * jnp.where(odd, ff, fb)
            kt = kk
        else:
            d1 = b_f - _boundary_rows(b_f, n, row8)
            d2 = _boundary_rows(b_b, n, row8) - bx_b
            qt = q * jnp.exp(jnp.minimum(d1, d2))
            kt = kk * jnp.exp(-jnp.maximum(d1, d2))
        products.append((n, _nt_dot(qt.astype(BF16), kt.astype(BF16))))
        n //= 2
    return products


def _assemble_scores(products, xor):
    p = None
    for n, pn in products:
        p = pn if p is None else jnp.where(xor < 2 * n, pn, p)
    return jnp.where(xor == 0, 0.0, p)


def _scan_fwd_kernel(q_ref, zf_ref, zb_ref, v_ref, lbf_ref, lbb_ref,
                     o_ref, qb_ref, kb_ref, tot_ref, vt_ref, st_ref, *, chunk):
    @pl.when(pl.program_id(2) == 0)
    def _():
        st_ref[...] = jnp.zeros_like(st_ref)

    c = chunk
    w = q_ref.shape[1]
    ri = lax.broadcasted_iota(jnp.int32, (c, c), 0)
    ci = lax.broadcasted_iota(jnp.int32, (c, c), 1)
    tri = (ri >= ci).astype(BF16)
    xor = ri ^ ci
    row = lax.broadcasted_iota(jnp.int32, (c, w), 0)
    row8 = lax.broadcasted_iota(jnp.int32, (SUBLANES, w), 0)
    c0f, c1f = _gate_consts(lbf_ref[...])
    c0b, c1b = _gate_consts(lbb_ref[...])

    nch = q_ref.shape[0] // c
    work = [dict(sl=pl.ds(ch * c, c), ch=ch) for ch in range(nch)]

    def stage_gates(s):
        sl = s["sl"]
        s["q"] = _silu(q_ref[sl, :])
        s["v"] = v_ref[sl, :]
        s["ff"], lf, s["kf"] = _gates(zf_ref[sl, :], c0f, c1f)
        s["fb"], s["lfb"], s["kb"] = _gates(zb_ref[sl, :], c0b, c1b)
        s["b_f"] = _cumsum_rows(lf, tri)
        s["b_b"] = _cumsum_rows(s["lfb"], tri)

    def stage_levels(s):
        q, v, kf, kb, b_f, b_b = s["q"], s["v"], s["kf"], s["kb"], s["b_f"], s["b_b"]
        bx_b = b_b - s["lfb"]
        s["products"] = _intra_scores(q, kf, kb, s["ff"], s["fb"], b_f, bx_b, b_b, row, row8, xor)
        s["diag"] = jnp.sum(q * (kf + kb), axis=-1, keepdims=True)
        b_last = b_f[c - 1:c, :]
        s["qh"] = (q * jnp.exp(b_f)).astype(BF16)
        kh = (kf * jnp.exp(b_last - b_f)).astype(BF16)
        vt = v.T.astype(BF16)
        vt_ref[:, s["sl"]] = vt
        s["upd"] = jnp.dot(vt, kh, preferred_element_type=F32)
        s["decay"] = jnp.exp(b_last)
        tot = b_b[c - 1:c, :]
        qb_ref[s["sl"], :] = (q * jnp.exp(tot - bx_b)).astype(BF16)
        kb_ref[s["sl"], :] = (kb * jnp.exp(bx_b)).astype(BF16)
        tot_ref[pl.ds(s["ch"] * SUBLANES, SUBLANES), :] = jnp.broadcast_to(jnp.exp(tot),
                                                                            (SUBLANES, w))

    def stage_scores(s):
        p = _assemble_scores(s.pop("products"), xor)
        s["pv"] = jnp.dot(p.astype(BF16), s["v"].astype(BF16), preferred_element_type=F32)

    stages = (stage_gates, stage_levels, stage_scores)
    for step in range(nch + len(stages) - 1):
        for k, stage in enumerate(stages):
            if 0 <= step - k < nch:
                stage(work[step - k])

    st = st_ref[...]
    for s in work:
        o_ref[s["sl"], :] = s["pv"] + s["diag"] * s["v"] + _nt_dot(s["qh"], st.astype(BF16))
        st = st * s["decay"] + s["upd"]
    st_ref[...] = st


def _scan_bwd_kernel(qb_ref, kb_ref, tot_ref, vt_ref, g_ref, o1_ref, hg_ref, o_ref, st_ref, *,
                     chunk, lookahead):
    @pl.when(pl.program_id(2) == 0)
    def _():
        st_ref[...] = jnp.zeros_like(st_ref)

    c = chunk
    nch = o1_ref.shape[0] // c
    hg = hg_ref[...]
    order = list(reversed(range(nch)))
    slices = {ch: pl.ds(ch * c, c) for ch in order}

    upd = {ch: jnp.dot(vt_ref[:, slices[ch]], kb_ref[slices[ch], :], preferred_element_type=F32)
           for ch in order}
    st = st_ref[...]
    st_in = {}
    for ch in order:
        st_in[ch] = st.astype(BF16)
        st = st * tot_ref[pl.ds(ch * SUBLANES, 1), :] + upd[ch]
    st_ref[...] = st

    inter = {}
    for i in range(nch + lookahead):
        if i < nch:
            ch = order[i]
            inter[ch] = _nt_dot(qb_ref[slices[ch], :], st_in.pop(ch))
        if i >= lookahead:
            ch = order[i - lookahead]
            o = o1_ref[slices[ch], :] + inter.pop(ch)
            o = o * lax.rsqrt(jnp.mean(o * o, axis=-1, keepdims=True) + NORM_EPS)
            o_ref[slices[ch], :] = (o * hg * _silu(g_ref[slices[ch], :])).astype(BF16)


def _scan_call(mid, lb_f, lb_b, hg_gain, *, row0, batch, seq, heads, cols, tc, tc_bwd):
    hd = HEAD_DIM
    nb = seq // tc
    rb0 = row0 // tc
    cq, czf, czb, cv, cg = (c // hd for c in cols)
    rows_per_tot = SCAN_CHUNK // SUBLANES
    hw = heads * hd

    def tok(col):
        return pl.BlockSpec((tc, hd), lambda b, h, j: (rb0 + b * nb + j, col + h))

    def own(rows):
        return pl.BlockSpec((rows, hd), lambda b, h, j: (b * nb + j, h))

    par = pl.BlockSpec((1, hd), lambda b, h, j: (0, h))
    o1, qb, kb, tot, vt = pl.pallas_call(
        functools.partial(_scan_fwd_kernel, chunk=SCAN_CHUNK),
        grid=(batch, heads, nb),
        in_specs=[tok(cq), tok(czf), tok(czb), tok(cv), par, par],
        out_specs=[own(tc), own(tc), own(tc), own(tc // rows_per_tot),
                   pl.BlockSpec((hd, tc), lambda b, h, j: (h, b * nb + j))],
        out_shape=[jax.ShapeDtypeStruct((batch * seq, hw), F32),
                   jax.ShapeDtypeStruct((batch * seq, hw), BF16),
                   jax.ShapeDtypeStruct((batch * seq, hw), BF16),
                   jax.ShapeDtypeStruct((batch * seq // rows_per_tot, hw), F32),
                   jax.ShapeDtypeStruct((hw, batch * seq), BF16)],
        scratch_shapes=[pltpu.VMEM((hd, hd), F32)],
        compiler_params=_cparams(("parallel", "parallel", "arbitrary")),
        name="hgrn2_fwd",
    )(mid, mid, mid, mid, lb_f, lb_b)

    nb2 = seq // tc_bwd
    rb2 = row0 // tc_bwd

    def own_rev(rows):
        return pl.BlockSpec((rows, hd), lambda b, h, j: (b * nb2 + nb2 - 1 - j, h))

    def tok_rev(col):
        return pl.BlockSpec((tc_bwd, hd), lambda b, h, j: (rb2 + b * nb2 + nb2 - 1 - j, col + h))

    return pl.pallas_call(
        functools.partial(_scan_bwd_kernel, chunk=SCAN_CHUNK, lookahead=2),
        grid=(batch, heads, nb2),
        in_specs=[own_rev(tc_bwd), own_rev(tc_bwd), own_rev(tc_bwd // rows_per_tot),
                  pl.BlockSpec((hd, tc_bwd), lambda b, h, j: (h, b * nb2 + nb2 - 1 - j)),
                  tok_rev(cg), own_rev(tc_bwd), par],
        out_specs=own_rev(tc_bwd),
        out_shape=jax.ShapeDtypeStruct((batch * seq, hw), BF16),
        scratch_shapes=[pltpu.VMEM((hd, hd), F32)],
        compiler_params=_cparams(("parallel", "parallel", "arbitrary")),
        name="hgrn2_bwd",
    )(qb, kb, tot, vt, mid, o1, hg_gain)


def _trunk(x, groups, params, *, ffn_tm, ffn_tf, proj_tm, proj_tn, out_tm, scan_tc, scan_tc_bwd):
    (ffn1_norm, ffn1_w_in, ffn1_w_down, mix_norm, w_in, lb_logits, hg_norm, w_fourier_out,
     w_hgrn_out, w_out, ffn2_norm, ffn2_w_in, ffn2_w_down, final_norm) = params
    depth, d = mix_norm.shape
    fw = w_fourier_out.shape[1]
    hw = w_hgrn_out.shape[1]
    heads = hw // HEAD_DIM
    d_ff = ffn1_w_down.shape[1]
    cols = tuple(j * hw for j in (0, 1, 2, 3, 4))

    def ffn_half(x, norm, w_in_l, w_down_l):
        return _ffn_call(x, norm[None, :], w_in_l[:, :d_ff].astype(BF16),
                         w_in_l[:, d_ff:].astype(BF16), w_down_l.astype(BF16),
                         tm=ffn_tm, tf=ffn_tf)

    lb = _lower_bound_call(lb_logits)
    dft = {seq: _dft_constants(seq, FOURIER_GROUP_DIM) for (_, _, seq) in groups}

    for l in range(depth):
        x = ffn_half(x, ffn1_norm[l], ffn1_w_in[l], ffn1_w_down[l])

        w_l = w_in[l]
        w_perm = jnp.concatenate([w_l[:, fw:fw + 5 * hw], w_l[:, :fw], w_l[:, fw + 5 * hw:]],
                                 axis=1).astype(BF16)
        mid, gates = _proj_call(x, mix_norm[l][None, :], w_perm, tm=proj_tm, tn=proj_tn,
                                n_gates=2 * d)
        fm_parts, ob_parts = [], []
        for (row0, batch, seq) in groups:
            fm_parts.append(_fourier_call(mid, dft[seq], row0=row0, batch=batch, seq=seq,
                                          col0=5 * hw, width=fw))
            ob_parts.append(_scan_call(mid, lb[0, l][None, :], lb[1, l][None, :],
                                       hg_norm[l][None, :], row0=row0, batch=batch, seq=seq,
                                       heads=heads, cols=cols, tc=scan_tc,
                                       tc_bwd=min(scan_tc_bwd, seq)))
        x = _mixout_call(x, fm_parts, ob_parts, gates, w_fourier_out[l].astype(BF16),
                         w_hgrn_out[l].astype(BF16), w_out[l].astype(BF16), tm=out_tm)

        x = ffn_half(x, ffn2_norm[l], ffn2_w_in[l], ffn2_w_down[l])

    return _norm_call(x, final_norm[None, :], tm=ffn_tm)


def kernel(x_prompt, x_sample, ffn1_norm, ffn1_w_in, ffn1_w_down, mix_norm, w_in, lb_logits, hg_norm, w_fourier_out, w_hgrn_out, w_out, ffn2_norm, ffn2_w_in, ffn2_w_down, final_norm):
    bp, sp, d = x_prompt.shape
    bs, ss, _ = x_sample.shape
    x = jnp.concatenate([x_prompt.reshape(bp * sp, d), x_sample.reshape(bs * ss, d)], axis=0)
    groups = [(0, bp, sp), (bp * sp, bs, ss)]
    params = (ffn1_norm, ffn1_w_in, ffn1_w_down, mix_norm, w_in, lb_logits, hg_norm,
              w_fourier_out, w_hgrn_out, w_out, ffn2_norm, ffn2_w_in, ffn2_w_down, final_norm)
    y = _trunk(x, groups, params, ffn_tm=768, ffn_tf=512, proj_tm=1024, proj_tn=1024,
               out_tm=256, scan_tc=1024, scan_tc_bwd=2048)
    return (y[:bp * sp].reshape(bp, sp, d), y[bp * sp:].reshape(bs, ss, d))
```

```python
import functools

import ml_dtypes
import numpy as np
import jax
import jax.numpy as jnp
from jax import lax
from jax.experimental import pallas as pl
from jax.experimental.pallas import tpu as pltpu

F32 = jnp.float32
BF16 = jnp.bfloat16

NORM_EPS = 1e-6
MIN_FORGET = 1e-30
LANES = 128
SUBLANES = 8
HEAD_DIM = 128
FOURIER_GROUP_DIM = 128
SCAN_CHUNK = 128
DFT_BLOCK = 128
VMEM_LIMIT = 56 * 1024 * 1024


def _cparams(sem):
    return pltpu.CompilerParams(dimension_semantics=sem, vmem_limit_bytes=VMEM_LIMIT)


def _rms_scale(x, gain):
    ms = jnp.mean(x * x, axis=-1, keepdims=True)
    return x * lax.rsqrt(ms + NORM_EPS) * gain


def _ffn_kernel(x_ref, gain_ref, wg_ref, wu_ref, wd_ref, o_ref, h_ref, *, last_cols):
    j = pl.program_id(1)
    last = pl.num_programs(1) - 1
    tf = wg_ref.shape[1]

    @pl.when(j == 0)
    def _():
        x = x_ref[...]
        h_ref[...] = _rms_scale(x, gain_ref[...]).astype(BF16)
        o_ref[...] = x

    def step(cols):
        h = h_ref[...]
        g = jnp.dot(h, wg_ref[:, :cols], preferred_element_type=F32)
        u = jnp.dot(h, wu_ref[:, :cols], preferred_element_type=F32)
        a = (g * jax.nn.sigmoid(g) * u * 0.5).astype(BF16)
        o_ref[...] += jnp.dot(a, wd_ref[:cols, :], preferred_element_type=F32)

    if last_cols == tf:
        step(tf)
    else:
        pl.when(j < last)(lambda: step(tf))
        pl.when(j == last)(lambda: step(last_cols))


def _ffn_call(x, gain, w_gate, w_up, w_down, *, tm, tf):
    t, d = x.shape
    f = w_down.shape[0]
    nf = pl.cdiv(f, tf)
    last_cols = f - (nf - 1) * tf
    assert last_cols % LANES == 0
    return pl.pallas_call(
        functools.partial(_ffn_kernel, last_cols=last_cols),
        grid=(t // tm, nf),
        in_specs=[
            pl.BlockSpec((tm, d), lambda i, j: (i, 0)),
            pl.BlockSpec((1, d), lambda i, j: (0, 0)),
            pl.BlockSpec((d, tf), lambda i, j: (0, j)),
            pl.BlockSpec((d, tf), lambda i, j: (0, j)),
            pl.BlockSpec((tf, d), lambda i, j: (j, 0)),
        ],
        out_specs=pl.BlockSpec((tm, d), lambda i, j: (i, 0)),
        out_shape=jax.ShapeDtypeStruct((t, d), F32),
        scratch_shapes=[pltpu.VMEM((tm, d), BF16)],
        compiler_params=_cparams(("parallel", "arbitrary")),
        name="ffn_half",
    )(x, gain, w_gate, w_up, w_down)


def _proj_kernel(x_ref, gain_ref, w_ref, act_ref, gate_ref, h_ref, *, j_gate):
    j = pl.program_id(1)

    @pl.when(j == 0)
    def _():
        h_ref[...] = _rms_scale(x_ref[...], gain_ref[...]).astype(BF16)

    def tile():
        return jnp.dot(h_ref[...], w_ref[...], preferred_element_type=F32)

    @pl.when(j < j_gate)
    def _():
        act_ref[...] = tile()

    @pl.when(j >= j_gate)
    def _():
        gate_ref[...] = tile()


def _proj_call(x, gain, w, *, tm, tn, n_gates):
    t, d = x.shape
    n = w.shape[1]
    n_act = n - n_gates
    j_gate = n_act // tn
    assert n_act % tn == 0 and n_gates % tn == 0
    return pl.pallas_call(
        functools.partial(_proj_kernel, j_gate=j_gate),
        grid=(t // tm, n // tn),
        in_specs=[
            pl.BlockSpec((tm, d), lambda i, j: (i, 0)),
            pl.BlockSpec((1, d), lambda i, j: (0, 0)),
            pl.BlockSpec((d, tn), lambda i, j: (0, j)),
        ],
        out_specs=[
            pl.BlockSpec((tm, tn), lambda i, j: (i, jnp.minimum(j, j_gate - 1))),
            pl.BlockSpec((tm, tn), lambda i, j: (i, jnp.maximum(j - j_gate, 0))),
        ],
        out_shape=[jax.ShapeDtypeStruct((t, n_act), F32),
                   jax.ShapeDtypeStruct((t, n_gates), F32)],
        scratch_shapes=[pltpu.VMEM((tm, d), BF16)],
        compiler_params=_cparams(("parallel", "arbitrary")),
        name="mixer_proj",
    )(x, gain, w)


def _sigmoid(x):
    return 0.5 + 0.5 * jnp.tanh(0.5 * x)


def _mixout_kernel(*refs, tile_starts):
    ng = len(tile_starts)
    x_ref = refs[0]
    fm_refs = refs[1:1 + ng]
    ob_refs = refs[1 + ng:1 + 2 * ng]
    ga_ref, gb_ref, wfo_ref, who_ref, wo_ref, o_ref = refs[1 + 2 * ng:]
    i = pl.program_id(0)
    fm, ob = fm_refs[0][...], ob_refs[0][...]
    for g in range(1, ng):
        in_g = i >= tile_starts[g]
        fm = jnp.where(in_g, fm_refs[g][...], fm)
        ob = jnp.where(in_g, ob_refs[g][...], ob)
    a = jnp.dot(fm.astype(BF16), wfo_ref[...], preferred_element_type=F32)
    b = jnp.dot(ob, who_ref[...], preferred_element_type=F32)
    m = (_sigmoid(ga_ref[...]) * a + _sigmoid(gb_ref[...]) * b).astype(BF16)
    o_ref[...] = x_ref[...] + jnp.dot(m, wo_ref[...], preferred_element_type=F32)


def _mixout_call(x, fm_parts, ob_parts, gates, w_fo, w_ho, w_out, *, tm):
    t, d = x.shape
    tile_starts, tile_counts, start = [], [], 0
    for part in fm_parts:
        tile_starts.append(start)
        tile_counts.append(part.shape[0] // tm)
        start += part.shape[0] // tm

    def group_spec(part, g):
        return pl.BlockSpec(
            (tm, part.shape[1]),
            lambda i: (jnp.clip(i - tile_starts[g], 0, tile_counts[g] - 1), 0))

    def resident(w):
        return pl.BlockSpec(w.shape, lambda i: (0, 0), pipeline_mode=pl.Buffered(1))

    return pl.pallas_call(
        functools.partial(_mixout_kernel, tile_starts=tuple(tile_starts)),
        grid=(t // tm,),
        in_specs=(
            [pl.BlockSpec((tm, d), lambda i: (i, 0))]
            + [group_spec(p, g) for g, p in enumerate(fm_parts)]
            + [group_spec(p, g) for g, p in enumerate(ob_parts)]
            + [pl.BlockSpec((tm, d), lambda i: (i, 0)), pl.BlockSpec((tm, d), lambda i: (i, 1)),
               resident(w_fo), resident(w_ho), resident(w_out)]),
        out_specs=pl.BlockSpec((tm, d), lambda i: (i, 0)),
        out_shape=jax.ShapeDtypeStruct((t, d), F32),
        compiler_params=_cparams(("parallel",)),
        name="mixer_out",
    )(x, *fm_parts, *ob_parts, gates, gates, w_fo, w_ho, w_out)


def _norm_kernel(x_ref, gain_ref, *o_refs, tile_starts):
    i = pl.program_id(0)
    bounds = tuple(tile_starts) + (pl.num_programs(0),)
    for g, o_ref in enumerate(o_refs):
        @pl.when((i >= bounds[g]) & (i < bounds[g + 1]))
        def _():
            o_ref[...] = _rms_scale(x_ref[...], gain_ref[...])


def _norm_call(x, gain, group_rows, *, tm):
    t, d = x.shape
    tile_starts, tile_counts, start = [], [], 0
    for rows in group_rows:
        tile_starts.append(start)
        tile_counts.append(rows // tm)
        start += rows // tm

    def group_spec(g):
        return pl.BlockSpec(
            (tm, d), lambda i: (jnp.clip(i - tile_starts[g], 0, tile_counts[g] - 1), 0))

    return pl.pallas_call(
        functools.partial(_norm_kernel, tile_starts=tuple(tile_starts)),
        grid=(t // tm,),
        in_specs=[pl.BlockSpec((tm, d), lambda i: (i, 0)), pl.BlockSpec((1, d), lambda i: (0, 0))],
        out_specs=[group_spec(g) for g in range(len(group_rows))],
        out_shape=[jax.ShapeDtypeStruct((rows, d), F32) for rows in group_rows],
        compiler_params=_cparams(("arbitrary",)),
        name="final_norm",
    )(x, gain)


def _lower_bound_kernel(logit_ref, o_ref, *, depth):
    for d0 in range(0, logit_ref.shape[0], depth):
        rows = [logit_ref[d0 + l:d0 + l + 1, :] for l in range(depth)]
        mx = functools.reduce(jnp.maximum, rows)
        es = [jnp.exp(r - mx) for r in rows]
        tot = functools.reduce(lambda a, b: a + b, es)
        ps = [e / tot for e in es]
        acc = ps[0]
        for l in range(depth):
            if l:
                acc = acc + ps[l]
            o_ref[d0 + l:d0 + l + 1, :] = jnp.maximum(acc - ps[0], 0.0)


def _lower_bound_call(lb_logits):
    dirs, depth, hw = lb_logits.shape
    lb = pl.pallas_call(
        functools.partial(_lower_bound_kernel, depth=depth),
        out_shape=jax.ShapeDtypeStruct((dirs * depth, hw), F32),
        name="lower_bounds",
    )(lb_logits.astype(F32).reshape(dirs * depth, hw))
    return lb.reshape(dirs, depth, hw)


def _split2_np(m):
    hi = m.astype(ml_dtypes.bfloat16).astype(np.float64)
    lo = m - hi
    return hi, lo


def _split2(x):
    hi = x.astype(BF16)
    lo = (x - hi.astype(F32)).astype(BF16)
    return hi, lo


def _dft_constants(s, gd):
    nr = DFT_BLOCK
    kr = np.arange(nr, dtype=np.float64)
    cr = np.cos(2 * np.pi * np.outer(kr, kr) / nr)
    sr = np.sin(2 * np.pi * np.outer(kr, kr) / nr)
    mr = np.block([[cr, sr], [-sr, cr]])
    mrh, mrl = _split2_np(mr)
    mrcat = np.concatenate([mrh, mrh, mrl], axis=1)

    kc = np.arange(gd, dtype=np.float64)
    scale = 1.0 / np.sqrt(float(s) * gd)
    cc = np.cos(2 * np.pi * np.outer(kc, kc) / gd) * scale
    sc = np.sin(2 * np.pi * np.outer(kc, kc) / gd) * scale
    mc = np.concatenate([cc, sc], axis=0)
    mch, mcl = _split2_np(mc)
    mccat = np.concatenate([mch, mch, mcl], axis=0)

    k2 = np.arange(s // nr, dtype=np.float64)
    ang = (2 * np.pi * np.outer(k2, kr) / s).reshape(s, 1)
    twc = jnp.broadcast_to(jnp.asarray(np.cos(ang), F32), (s, LANES))
    tws = jnp.broadcast_to(jnp.asarray(np.sin(ang), F32), (s, LANES))
    return jnp.asarray(mrcat, BF16), jnp.asarray(mccat, BF16), twc, tws


def _c_add(a, b):
    return b if a is None else a if b is None else a + b


def _c_sub(a, b):
    return a if b is None else -b if a is None else a - b


def _fft_blocks(xs):
    n = len(xs)
    if n == 1:
        return xs
    even, odd = _fft_blocks(xs[0::2]), _fft_blocks(xs[1::2])
    out = [None] * n
    for k in range(n // 2):
        o_r, o_i = odd[k]
        if k == 0:
            t_r, t_i = o_r, o_i
        elif 4 * k == n:
            t_r, t_i = o_i, (None if o_r is None else -o_r)
        else:
            wr, wi = float(np.cos(2 * np.pi * k / n)), float(np.sin(2 * np.pi * k / n))
            t_r = _c_add(None if o_r is None else wr * o_r, None if o_i is None else wi * o_i)
            t_i = _c_sub(None if o_i is None else wr * o_i, None if o_r is None else wi * o_r)
        e_r, e_i = even[k]
        out[k] = (_c_add(e_r, t_r), _c_add(e_i, t_i))
        out[k + n // 2] = (_c_sub(e_r, t_r), _c_sub(e_i, t_i))
    return out


def _dft_kernel(x_ref, twc_ref, tws_ref, mr_ref, mc_ref, o_ref):
    nr = DFT_BLOCK
    nb = x_ref.shape[0] // nr

    def rows(ref, j):
        return ref[pl.ds(j * nr, nr), :]

    u = _fft_blocks([(rows(x_ref, j), None) for j in range(nb)])
    for k2 in range(nb):
        u_r, u_i = u[k2]
        u_r = jnp.zeros((nr, x_ref.shape[1]), F32) if u_r is None else u_r
        u_i = jnp.zeros_like(u_r) if u_i is None else u_i
        if k2 == 0:
            t_r, t_i = u_r, u_i
        else:
            c, s = rows(twc_ref, k2), rows(tws_ref, k2)
            t_r, t_i = u_r * c + u_i * s, u_i * c - u_r * s
        th, tl = _split2(jnp.concatenate([t_r, t_i], axis=0))
        v = jnp.dot(mr_ref[...], jnp.concatenate([th, tl, th], axis=0),
                    preferred_element_type=F32)
        vh, vl = _split2(jnp.concatenate([v[:nr], v[nr:]], axis=1))
        o_ref[0, k2] = jnp.dot(jnp.concatenate([vh, vl, vh], axis=1), mc_ref[...],
                               preferred_element_type=F32)


def _fourier_call(act, consts, *, row0, batch, seq, col0, width):
    gd = FOURIER_GROUP_DIM
    nb = seq // DFT_BLOCK
    mrcat, mccat, twc, tws = consts

    def resident(a):
        return pl.BlockSpec(a.shape, lambda b, g: (0, 0), pipeline_mode=pl.Buffered(1))

    y = pl.pallas_call(
        _dft_kernel,
        grid=(batch, width // gd),
        in_specs=[pl.BlockSpec((seq, gd), lambda b, g: (row0 // seq + b, col0 // gd + g)),
                  resident(twc), resident(tws), resident(mrcat), resident(mccat)],
        out_specs=pl.BlockSpec((1, nb, DFT_BLOCK, gd), lambda b, g: (b, 0, 0, g)),
        out_shape=jax.ShapeDtypeStruct((batch, nb, DFT_BLOCK, width), F32),
        compiler_params=_cparams(("parallel", "parallel")),
        name="dft2d",
    )(act, twc, tws, mrcat, mccat)
    return y.transpose(0, 2, 1, 3).reshape(batch * seq, width)


def _silu(x):
    h = 0.5 * x
    return h + h * jnp.tanh(h)


def _gate_consts(lb):
    c1 = 0.5 * (1.0 - lb)
    return lb + c1, c1


def _gates(z, c0, c1):
    f = c0 + c1 * jnp.tanh(0.5 * z)
    f_floor = jnp.maximum(f, MIN_FORGET)
    return f_floor, jnp.log(f_floor), 1.0 - f


def _cumsum_rows(x, tri):
    w = x.shape[1]
    hi = x.astype(BF16)
    lo = (x - hi.astype(F32)).astype(BF16)
    y = jnp.dot(tri, jnp.concatenate([hi, lo], axis=1), preferred_element_type=F32)
    return y[:, :w] + y[:, w:]


def _boundary_rows(x, n, row8):
    c, w = x.shape
    parts = []
    if 2 * n >= SUBLANES:
        for p in range(c // (2 * n)):
            r = p * 2 * n + n - 1
            parts.append(jnp.broadcast_to(x[r:r + 1, :], (2 * n, w)))
    else:
        lo = row8 < 4
        for p in range(c // SUBLANES):
            r = p * SUBLANES
            a = jnp.broadcast_to(x[r + 1:r + 2, :], (SUBLANES, w))
            b = jnp.broadcast_to(x[r + 5:r + 6, :], (SUBLANES, w))
            parts.append(jnp.where(lo, a, b))
    return jnp.concatenate(parts, axis=0)


def _nt_dot(a, b):
    return lax.dot_general(a, b, (((1,), (1,)), ((), ())), preferred_element_type=F32)


def _intra_scores(q, kf, kb, ff, fb, b_f, bx_b, b_b, row, row8, xor):
    c = q.shape[0]
    products = []
    n = c // 2
    while n >= 1:
        if n >= SUBLANES:
            qt_parts, kt_parts = [], []
            for lo in range(0, c, 2 * n):
                mid, hi = lo + n, lo + 2 * n
                bf_m, bb_m = b_f[mid - 1:mid, :], b_b[mid - 1:mid, :]
                qt_parts.append(q[lo:mid] * jnp.exp(bb_m - bx_b[lo:mid]))
                kt_parts.append(kf[lo:mid] * jnp.exp(bf_m - b_f[lo:mid]))
                qt_parts.append(q[mid:hi] * jnp.exp(b_f[mid:hi] - bf_m))
                kt_parts.append(kb[mid:hi] * jnp.exp(bx_b[mid:hi] - bb_m))
            qt, kt = jnp.concatenate(qt_parts, axis=0), jnp.concatenate(kt_parts, axis=0)
        else:
            odd = (row & n) != 0
            kk = jnp.where(odd, kb, kf)
            if n == 1:
                qt = q * jnp.where(odd, ff, fb)
                kt = kk
            else:
                d1 = b_f - _boundary_rows(b_f, n, row8)
                d2 = _boundary_rows(b_b, n, row8) - bx_b
                qt = q * jnp.exp(jnp.minimum(d1, d2))
                kt = kk * jnp.exp(-jnp.maximum(d1, d2))
        products.append((n, _nt_dot(qt.astype(BF16), kt.astype(BF16))))
        n //= 2
    return products


def _assemble_scores(products, xor):
    p = None
    for n, pn in products:
        p = pn if p is None else jnp.where(xor < 2 * n, pn, p)
    return jnp.where(xor == 0, 0.0, p)


def _scan_fwd_kernel(q_ref, zf_ref, zb_ref, v_ref, lbf_ref, lbb_ref,
                     o_ref, qb_ref, kb_ref, tot_ref, vt_ref, st_ref, *, chunk):
    @pl.when(pl.program_id(2) == 0)
    def _():
        st_ref[...] = jnp.zeros_like(st_ref)

    c = chunk
    w = q_ref.shape[1]
    ri = lax.broadcasted_iota(jnp.int32, (c, c), 0)
    ci = lax.broadcasted_iota(jnp.int32, (c, c), 1)
    tri = (ri >= ci).astype(BF16)
    xor = ri ^ ci
    row = lax.broadcasted_iota(jnp.int32, (c, w), 0)
    row8 = lax.broadcasted_iota(jnp.int32, (SUBLANES, w), 0)
    c0f, c1f = _gate_consts(lbf_ref[...])
    c0b, c1b = _gate_consts(lbb_ref[...])

    nch = q_ref.shape[0] // c
    work = [dict(sl=pl.ds(ch * c, c), ch=ch) for ch in range(nch)]

    def stage_gates(s):
        sl = s["sl"]
        s["q"] = _silu(q_ref[sl, :])
        s["v"] = v_ref[sl, :]
        s["ff"], lf, s["kf"] = _gates(zf_ref[sl, :], c0f, c1f)
        s["fb"], s["lfb"], s["kb"] = _gates(zb_ref[sl, :], c0b, c1b)
        s["b_f"] = _cumsum_rows(lf, tri)
        s["b_b"] = _cumsum_rows(s["lfb"], tri)

    def stage_levels(s):
        q, v, kf, kb, b_f, b_b = s["q"], s["v"], s["kf"], s["kb"], s["b_f"], s["b_b"]
        bx_b = b_b - s["lfb"]
        s["products"] = _intra_scores(q, kf, kb, s["ff"], s["fb"], b_f, bx_b, b_b, row, row8, xor)
        s["diag"] = jnp.sum(q * (kf + kb), axis=-1, keepdims=True)
        b_last = b_f[c - 1:c, :]
        s["qh"] = (q * jnp.exp(b_f)).astype(BF16)
        kh = (kf * jnp.exp(b_last - b_f)).astype(BF16)
        vt = v.T.astype(BF16)
        vt_ref[:, s["sl"]] = vt
        s["upd"] = jnp.dot(vt, kh, preferred_element_type=F32)
        s["decay"] = jnp.exp(b_last)
        tot = b_b[c - 1:c, :]
        qb_ref[s["sl"], :] = (q * jnp.exp(tot - bx_b)).astype(BF16)
        kb_ref[s["sl"], :] = (kb * jnp.exp(bx_b)).astype(BF16)
        tot_ref[pl.ds(s["ch"] * SUBLANES, SUBLANES), :] = jnp.broadcast_to(jnp.exp(tot),
                                                                            (SUBLANES, w))

    def stage_scores(s):
        p = _assemble_scores(s.pop("products"), xor)
        s["pv"] = jnp.dot(p.astype(BF16), s["v"].astype(BF16), preferred_element_type=F32)

    stages = (stage_gates, stage_levels, stage_scores)
    for step in range(nch + len(stages) - 1):
        for k, stage in enumerate(stages):
            if 0 <= step - k < nch:
                stage(work[step - k])

    st = st_ref[...]
    for s in work:
        o_ref[s["sl"], :] = s["pv"] + s["diag"] * s["v"] + _nt_dot(s["qh"], st.astype(BF16))
        st = st * s["decay"] + s["upd"]
    st_ref[...] = st


def _scan_bwd_kernel(qb_ref, kb_ref, tot_ref, vt_ref, g_ref, o1_ref, hg_ref, o_ref, st_ref, *,
                     chunk, lookahead):
    @pl.when(pl.program_id(2) == 0)
    def _():
        st_ref[...] = jnp.zeros_like(st_ref)

    c = chunk
    nch = o1_ref.shape[0] // c
    hg = hg_ref[...]
    order = list(reversed(range(nch)))
    slices = {ch: pl.ds(ch * c, c) for ch in order}

    upd = {ch: jnp.dot(vt_ref[:, slices[ch]], kb_ref[slices[ch], :], preferred_element_type=F32)
           for ch in order}
    st = st_ref[...]
    st_in = {}
    for ch in order:
        st_in[ch] = st.astype(BF16)
        st = st * tot_ref[pl.ds(ch * SUBLANES, 1), :] + upd[ch]
    st_ref[...] = st

    inter = {}
    for i in range(nch + lookahead):
        if i < nch:
            ch = order[i]
            inter[ch] = _nt_dot(qb_ref[slices[ch], :], st_in.pop(ch))
        if i >= lookahead:
            ch = order[i - lookahead]
            o = o1_ref[slices[ch], :] + inter.pop(ch)
            o = o * lax.rsqrt(jnp.mean(o * o, axis=-1, keepdims=True) + NORM_EPS)
            o_ref[slices[ch], :] = (o * hg * _silu(g_ref[slices[ch], :])).astype(BF16)


def _scan_call(mid, lb_f, lb_b, hg_gain, *, row0, batch, seq, heads, cols, tc, tc_bwd):
    hd = HEAD_DIM
    nb = seq // tc
    rb0 = row0 // tc
    cq, czf, czb, cv, cg = (c // hd for c in cols)
    rows_per_tot = SCAN_CHUNK // SUBLANES
    hw = heads * hd

    def tok(col):
        return pl.BlockSpec((tc, hd), lambda b, h, j: (rb0 + b * nb + j, col + h))

    def own(rows):
        return pl.BlockSpec((rows, hd), lambda b, h, j: (b * nb + j, h))

    par = pl.BlockSpec((1, hd), lambda b, h, j: (0, h))
    o1, qb, kb, tot, vt = pl.pallas_call(
        functools.partial(_scan_fwd_kernel, chunk=SCAN_CHUNK),
        grid=(batch, heads, nb),
        in_specs=[tok(cq), tok(czf), tok(czb), tok(cv), par, par],
        out_specs=[own(tc), own(tc), own(tc), own(tc // rows_per_tot),
                   pl.BlockSpec((hd, tc), lambda b, h, j: (h, b * nb + j))],
        out_shape=[jax.ShapeDtypeStruct((batch * seq, hw), F32),
                   jax.ShapeDtypeStruct((batch * seq, hw), BF16),
                   jax.ShapeDtypeStruct((batch * seq, hw), BF16),
                   jax.ShapeDtypeStruct((batch * seq // rows_per_tot, hw), F32),
                   jax.ShapeDtypeStruct((hw, batch * seq), BF16)],
        scratch_shapes=[pltpu.VMEM((hd, hd), F32)],
        compiler_params=_cparams(("parallel", "parallel", "arbitrary")),
        name="hgrn2_fwd",
    )(mid, mid, mid, mid, lb_f, lb_b)

    nb2 = seq // tc_bwd
    rb2 = row0 // tc_bwd

    def own_rev(rows):
        return pl.BlockSpec((rows, hd), lambda b, h, j: (b * nb2 + nb2 - 1 - j, h))

    def tok_rev(col):
        return pl.BlockSpec((tc_bwd, hd), lambda b, h, j: (rb2 + b * nb2 + nb2 - 1 - j, col + h))

    return pl.pallas_call(
        functools.partial(_scan_bwd_kernel, chunk=SCAN_CHUNK, lookahead=2),
        grid=(batch, heads, nb2),
        in_specs=[own_rev(tc_bwd), own_rev(tc_bwd), own_rev(tc_bwd // rows_per_tot),
                  pl.BlockSpec((hd, tc_bwd), lambda b, h, j: (h, b * nb2 + nb2 - 1 - j)),
                  tok_rev(cg), own_rev(tc_bwd), par],
        out_specs=own_rev(tc_bwd),
        out_shape=jax.ShapeDtypeStruct((batch * seq, hw), BF16),
        scratch_shapes=[pltpu.VMEM((hd, hd), F32)],
        compiler_params=_cparams(("parallel", "parallel", "arbitrary")),
        name="hgrn2_bwd",
    )(qb, kb, tot, vt, mid, o1, hg_gain)


def _trunk(x, groups, params, *, ffn_tm, ffn_tf, proj_tm, proj_tn, out_tm, scan_tc, scan_tc_bwd):
    (ffn1_norm, ffn1_w_in, ffn1_w_down, mix_norm, w_in, lb_logits, hg_norm, w_fourier_out,
     w_hgrn_out, w_out, ffn2_norm, ffn2_w_in, ffn2_w_down, final_norm) = params
    depth, d = mix_norm.shape
    fw = w_fourier_out.shape[1]
    hw = w_hgrn_out.shape[1]
    heads = hw // HEAD_DIM
    d_ff = ffn1_w_down.shape[1]
    cols = tuple(j * hw for j in (0, 1, 2, 3, 4))

    def ffn_half(x, norm, w_in_l, w_down_l):
        return _ffn_call(x, norm[None, :], w_in_l[:, :d_ff].astype(BF16),
                         w_in_l[:, d_ff:].astype(BF16), w_down_l.astype(BF16),
                         tm=ffn_tm, tf=ffn_tf)

    lb = _lower_bound_call(lb_logits)
    dft = {seq: _dft_constants(seq, FOURIER_GROUP_DIM) for (_, _, seq) in groups}

    for l in range(depth):
        x = ffn_half(x, ffn1_norm[l], ffn1_w_in[l], ffn1_w_down[l])

        w_l = w_in[l]
        w_perm = jnp.concatenate([w_l[:, fw:fw + 5 * hw], w_l[:, :fw], w_l[:, fw + 5 * hw:]],
                                 axis=1).astype(BF16)
        mid, gates = _proj_call(x, mix_norm[l][None, :], w_perm, tm=proj_tm, tn=proj_tn,
                                n_gates=2 * d)
        fm_parts, ob_parts = [], []
        for (row0, batch, seq) in groups:
            fm_parts.append(_fourier_call(mid, dft[seq], row0=row0, batch=batch, seq=seq,
                                          col0=5 * hw, width=fw))
            ob_parts.append(_scan_call(mid, lb[0, l][None, :], lb[1, l][None, :],
                                       hg_norm[l][None, :], row0=row0, batch=batch, seq=seq,
                                       heads=heads, cols=cols, tc=scan_tc,
                                       tc_bwd=min(scan_tc_bwd, seq)))
        x = _mixout_call(x, fm_parts, ob_parts, gates, w_fourier_out[l].astype(BF16),
                         w_hgrn_out[l].astype(BF16), w_out[l].astype(BF16), tm=out_tm)

        x = ffn_half(x, ffn2_norm[l], ffn2_w_in[l], ffn2_w_down[l])

    return _norm_call(x, final_norm[None, :], [batch * seq for (_, batch, seq) in groups],
                      tm=out_tm)


def kernel(x_prompt, x_sample, ffn1_norm, ffn1_w_in, ffn1_w_down, mix_norm, w_in, lb_logits, hg_norm, w_fourier_out, w_hgrn_out, w_out, ffn2_norm, ffn2_w_in, ffn2_w_down, final_norm):
    bp, sp, d = x_prompt.shape
    bs, ss, _ = x_sample.shape
    x = jnp.concatenate([x_prompt.reshape(bp * sp, d), x_sample.reshape(bs * ss, d)], axis=0)
    groups = [(0, bp, sp), (bp * sp, bs, ss)]
    params = (ffn1_norm, ffn1_w_in, ffn1_w_down, mix_norm, w_in, lb_logits, hg_norm,
              w_fourier_out, w_hgrn_out, w_out, ffn2_norm, ffn2_w_in, ffn2_w_down, final_norm)
    y_prompt, y_sample = _trunk(x, groups, params, ffn_tm=1024, ffn_tf=512, proj_tm=1024,
                                proj_tn=1024, out_tm=256, scan_tc=1024, scan_tc_bwd=2048)
    return (y_prompt.reshape(bp, sp, d), y_sample.reshape(bs, ss, d))
```

```python
import functools

import ml_dtypes
import numpy as np
import jax
import jax.numpy as jnp
from jax import lax
from jax.experimental import pallas as pl
from jax.experimental.pallas import tpu as pltpu

F32 = jnp.float32
BF16 = jnp.bfloat16

NORM_EPS = 1e-6
MIN_FORGET = 1e-30
LANES = 128
SUBLANES = 8
HEAD_DIM = 128
FOURIER_GROUP_DIM = 128
SCAN_CHUNK = 128
DFT_BLOCK = 128
VMEM_LIMIT = 56 * 1024 * 1024


def _cparams(sem):
    return pltpu.CompilerParams(dimension_semantics=sem, vmem_limit_bytes=VMEM_LIMIT)


def _rms_scale(x, gain):
    ms = jnp.mean(x * x, axis=-1, keepdims=True)
    return x * lax.rsqrt(ms + NORM_EPS) * gain


def _ffn_kernel(x_ref, gain_ref, wg_ref, wu_ref, wd_ref, o_ref, h_ref, *, last_cols):
    j = pl.program_id(1)
    last = pl.num_programs(1) - 1
    tf = wg_ref.shape[1]

    @pl.when(j == 0)
    def _():
        x = x_ref[...]
        h_ref[...] = _rms_scale(x, gain_ref[...]).astype(BF16)
        o_ref[...] = x

    def step(cols):
        h = h_ref[...]
        g = jnp.dot(h, wg_ref[:, :cols], preferred_element_type=F32)
        u = jnp.dot(h, wu_ref[:, :cols], preferred_element_type=F32)
        a = (g * jax.nn.sigmoid(g) * u * 0.5).astype(BF16)
        o_ref[...] += jnp.dot(a, wd_ref[:cols, :], preferred_element_type=F32)

    if last_cols == tf:
        step(tf)
    else:
        pl.when(j < last)(lambda: step(tf))
        pl.when(j == last)(lambda: step(last_cols))


def _ffn_call(x, gain, w_gate, w_up, w_down, *, tm, tf):
    t, d = x.shape
    f = w_down.shape[0]
    nf = pl.cdiv(f, tf)
    last_cols = f - (nf - 1) * tf
    assert last_cols % LANES == 0
    return pl.pallas_call(
        functools.partial(_ffn_kernel, last_cols=last_cols),
        grid=(t // tm, nf),
        in_specs=[
            pl.BlockSpec((tm, d), lambda i, j: (i, 0)),
            pl.BlockSpec((1, d), lambda i, j: (0, 0)),
            pl.BlockSpec((d, tf), lambda i, j: (0, j)),
            pl.BlockSpec((d, tf), lambda i, j: (0, j)),
            pl.BlockSpec((tf, d), lambda i, j: (j, 0)),
        ],
        out_specs=pl.BlockSpec((tm, d), lambda i, j: (i, 0)),
        out_shape=jax.ShapeDtypeStruct((t, d), F32),
        scratch_shapes=[pltpu.VMEM((tm, d), BF16)],
        compiler_params=_cparams(("parallel", "arbitrary")),
        name="ffn_half",
    )(x, gain, w_gate, w_up, w_down)


def _proj_kernel(x_ref, gain_ref, w_ref, act_ref, gate_ref, h_ref, *, j_gate):
    j = pl.program_id(1)

    @pl.when(j == 0)
    def _():
        h_ref[...] = _rms_scale(x_ref[...], gain_ref[...]).astype(BF16)

    def tile():
        return jnp.dot(h_ref[...], w_ref[...], preferred_element_type=F32)

    @pl.when(j < j_gate)
    def _():
        act_ref[...] = tile()

    @pl.when(j >= j_gate)
    def _():
        gate_ref[...] = tile()


def _proj_call(x, gain, w, *, tm, tn, n_gates):
    t, d = x.shape
    n = w.shape[1]
    n_act = n - n_gates
    j_gate = n_act // tn
    assert n_act % tn == 0 and n_gates % tn == 0
    return pl.pallas_call(
        functools.partial(_proj_kernel, j_gate=j_gate),
        grid=(t // tm, n // tn),
        in_specs=[
            pl.BlockSpec((tm, d), lambda i, j: (i, 0)),
            pl.BlockSpec((1, d), lambda i, j: (0, 0)),
            pl.BlockSpec((d, tn), lambda i, j: (0, j)),
        ],
        out_specs=[
            pl.BlockSpec((tm, tn), lambda i, j: (i, jnp.minimum(j, j_gate - 1))),
            pl.BlockSpec((tm, tn), lambda i, j: (i, jnp.maximum(j - j_gate, 0))),
        ],
        out_shape=[jax.ShapeDtypeStruct((t, n_act), F32),
                   jax.ShapeDtypeStruct((t, n_gates), F32)],
        scratch_shapes=[pltpu.VMEM((tm, d), BF16)],
        compiler_params=_cparams(("parallel", "arbitrary")),
        name="mixer_proj",
    )(x, gain, w)


def _sigmoid(x):
    return 0.5 + 0.5 * jnp.tanh(0.5 * x)


def _mixout_kernel(*refs, tile_starts):
    ng = len(tile_starts)
    x_ref = refs[0]
    fm_refs = refs[1:1 + ng]
    ob_refs = refs[1 + ng:1 + 2 * ng]
    ga_ref, gb_ref, wfo_ref, who_ref, wo_ref, o_ref = refs[1 + 2 * ng:]
    i = pl.program_id(0)
    fm, ob = fm_refs[0][...], ob_refs[0][...]
    for g in range(1, ng):
        in_g = i >= tile_starts[g]
        fm = jnp.where(in_g, fm_refs[g][...], fm)
        ob = jnp.where(in_g, ob_refs[g][...], ob)
    a = jnp.dot(fm.astype(BF16), wfo_ref[...], preferred_element_type=F32)
    b = jnp.dot(ob, who_ref[...], preferred_element_type=F32)
    m = (_sigmoid(ga_ref[...]) * a + _sigmoid(gb_ref[...]) * b).astype(BF16)
    o_ref[...] = x_ref[...] + jnp.dot(m, wo_ref[...], preferred_element_type=F32)


def _mixout_call(x, fm_parts, ob_parts, gates, w_fo, w_ho, w_out, *, tm):
    t, d = x.shape
    tile_starts, tile_counts, start = [], [], 0
    for part in fm_parts:
        tile_starts.append(start)
        tile_counts.append(part.shape[0] // tm)
        start += part.shape[0] // tm

    def group_spec(part, g):
        return pl.BlockSpec(
            (tm, part.shape[1]),
            lambda i: (jnp.clip(i - tile_starts[g], 0, tile_counts[g] - 1), 0))

    def resident(w):
        return pl.BlockSpec(w.shape, lambda i: (0, 0), pipeline_mode=pl.Buffered(1))

    return pl.pallas_call(
        functools.partial(_mixout_kernel, tile_starts=tuple(tile_starts)),
        grid=(t // tm,),
        in_specs=(
            [pl.BlockSpec((tm, d), lambda i: (i, 0))]
            + [group_spec(p, g) for g, p in enumerate(fm_parts)]
            + [group_spec(p, g) for g, p in enumerate(ob_parts)]
            + [pl.BlockSpec((tm, d), lambda i: (i, 0)), pl.BlockSpec((tm, d), lambda i: (i, 1)),
               resident(w_fo), resident(w_ho), resident(w_out)]),
        out_specs=pl.BlockSpec((tm, d), lambda i: (i, 0)),
        out_shape=jax.ShapeDtypeStruct((t, d), F32),
        compiler_params=_cparams(("parallel",)),
        name="mixer_out",
    )(x, *fm_parts, *ob_parts, gates, gates, w_fo, w_ho, w_out)


def _norm_kernel(x_ref, gain_ref, *o_refs, tile_starts):
    i = pl.program_id(0)
    bounds = tuple(tile_starts) + (pl.num_programs(0),)
    for g, o_ref in enumerate(o_refs):
        @pl.when((i >= bounds[g]) & (i < bounds[g + 1]))
        def _():
            o_ref[...] = _rms_scale(x_ref[...], gain_ref[...])


def _norm_call(x, gain, group_rows, *, tm):
    t, d = x.shape
    tile_starts, tile_counts, start = [], [], 0
    for rows in group_rows:
        tile_starts.append(start)
        tile_counts.append(rows // tm)
        start += rows // tm

    def group_spec(g):
        return pl.BlockSpec(
            (tm, d), lambda i: (jnp.clip(i - tile_starts[g], 0, tile_counts[g] - 1), 0))

    return pl.pallas_call(
        functools.partial(_norm_kernel, tile_starts=tuple(tile_starts)),
        grid=(t // tm,),
        in_specs=[pl.BlockSpec((tm, d), lambda i: (i, 0)), pl.BlockSpec((1, d), lambda i: (0, 0))],
        out_specs=[group_spec(g) for g in range(len(group_rows))],
        out_shape=[jax.ShapeDtypeStruct((rows, d), F32) for rows in group_rows],
        compiler_params=_cparams(("arbitrary",)),
        name="final_norm",
    )(x, gain)


def _lower_bound_kernel(logit_ref, o_ref, *, depth):
    for d0 in range(0, logit_ref.shape[0], depth):
        rows = [logit_ref[d0 + l:d0 + l + 1, :] for l in range(depth)]
        mx = functools.reduce(jnp.maximum, rows)
        es = [jnp.exp(r - mx) for r in rows]
        tot = functools.reduce(lambda a, b: a + b, es)
        ps = [e / tot for e in es]
        acc = ps[0]
        for l in range(depth):
            if l:
                acc = acc + ps[l]
            o_ref[d0 + l:d0 + l + 1, :] = jnp.maximum(acc - ps[0], 0.0)


def _lower_bound_call(lb_logits):
    dirs, depth, hw = lb_logits.shape
    lb = pl.pallas_call(
        functools.partial(_lower_bound_kernel, depth=depth),
        out_shape=jax.ShapeDtypeStruct((dirs * depth, hw), F32),
        name="lower_bounds",
    )(lb_logits.astype(F32).reshape(dirs * depth, hw))
    return lb.reshape(dirs, depth, hw)


def _split2_np(m):
    hi = m.astype(ml_dtypes.bfloat16).astype(np.float64)
    lo = m - hi
    return hi, lo


def _split2(x):
    hi = x.astype(BF16)
    lo = (x - hi.astype(F32)).astype(BF16)
    return hi, lo


def _dft_constants(s, gd):
    nr = DFT_BLOCK
    kr = np.arange(nr, dtype=np.float64)
    cr = np.cos(2 * np.pi * np.outer(kr, kr) / nr)
    sr = np.sin(2 * np.pi * np.outer(kr, kr) / nr)
    mr = np.block([[cr, sr], [-sr, cr]])
    mrh, mrl = _split2_np(mr)
    mrcat = np.concatenate([mrh, mrh, mrl], axis=1)

    kc = np.arange(gd, dtype=np.float64)
    scale = 1.0 / np.sqrt(float(s) * gd)
    cc = np.cos(2 * np.pi * np.outer(kc, kc) / gd) * scale
    sc = np.sin(2 * np.pi * np.outer(kc, kc) / gd) * scale
    mc = np.concatenate([cc, sc], axis=0)
    mch, mcl = _split2_np(mc)
    mccat = np.concatenate([mch, mch, mcl], axis=0)

    k2 = np.arange(s // nr, dtype=np.float64)
    ang = (2 * np.pi * np.outer(k2, kr) / s).reshape(s, 1)
    twc = jnp.broadcast_to(jnp.asarray(np.cos(ang), F32), (s, LANES))
    tws = jnp.broadcast_to(jnp.asarray(np.sin(ang), F32), (s, LANES))
    return jnp.asarray(mrcat, BF16), jnp.asarray(mccat, BF16), twc, tws


def _c_add(a, b):
    return b if a is None else a if b is None else a + b


def _c_sub(a, b):
    return a if b is None else -b if a is None else a - b


def _fft_blocks(xs):
    n = len(xs)
    if n == 1:
        return xs
    even, odd = _fft_blocks(xs[0::2]), _fft_blocks(xs[1::2])
    out = [None] * n
    for k in range(n // 2):
        o_r, o_i = odd[k]
        if k == 0:
            t_r, t_i = o_r, o_i
        elif 4 * k == n:
            t_r, t_i = o_i, (None if o_r is None else -o_r)
        else:
            wr, wi = float(np.cos(2 * np.pi * k / n)), float(np.sin(2 * np.pi * k / n))
            t_r = _c_add(None if o_r is None else wr * o_r, None if o_i is None else wi * o_i)
            t_i = _c_sub(None if o_i is None else wr * o_i, None if o_r is None else wi * o_r)
        e_r, e_i = even[k]
        out[k] = (_c_add(e_r, t_r), _c_add(e_i, t_i))
        out[k + n // 2] = (_c_sub(e_r, t_r), _c_sub(e_i, t_i))
    return out


def _dft_kernel(x_ref, twc_ref, tws_ref, mr_ref, mc_ref, o_ref):
    nr = DFT_BLOCK
    nb = x_ref.shape[0] // nr

    def rows(ref, j):
        return ref[pl.ds(j * nr, nr), :]

    u = _fft_blocks([(rows(x_ref, j), None) for j in range(nb)])
    for k2 in range(nb):
        u_r, u_i = u[k2]
        u_r = jnp.zeros((nr, x_ref.shape[1]), F32) if u_r is None else u_r
        u_i = jnp.zeros_like(u_r) if u_i is None else u_i
        if k2 == 0:
            t_r, t_i = u_r, u_i
        else:
            c, s = rows(twc_ref, k2), rows(tws_ref, k2)
            t_r, t_i = u_r * c + u_i * s, u_i * c - u_r * s
        th, tl = _split2(jnp.concatenate([t_r, t_i], axis=0))
        v = jnp.dot(mr_ref[...], jnp.concatenate([th, tl, th], axis=0),
                    preferred_element_type=F32)
        vh, vl = _split2(jnp.concatenate([v[:nr], v[nr:]], axis=1))
        o_ref[0, :, k2, :] = jnp.dot(jnp.concatenate([vh, vl, vh], axis=1), mc_ref[...],
                                     preferred_element_type=F32)


def _fourier_call(act, consts, *, row0, batch, seq, col0, width):
    gd = FOURIER_GROUP_DIM
    nb = seq // DFT_BLOCK
    mrcat, mccat, twc, tws = consts

    def resident(a):
        return pl.BlockSpec(a.shape, lambda b, g: (0, 0), pipeline_mode=pl.Buffered(1))

    y = pl.pallas_call(
        _dft_kernel,
        grid=(batch, width // gd),
        in_specs=[pl.BlockSpec((seq, gd), lambda b, g: (row0 // seq + b, col0 // gd + g)),
                  resident(twc), resident(tws), resident(mrcat), resident(mccat)],
        out_specs=pl.BlockSpec((1, DFT_BLOCK, nb, gd), lambda b, g: (b, 0, 0, g)),
        out_shape=jax.ShapeDtypeStruct((batch, DFT_BLOCK, nb, width), F32),
        compiler_params=_cparams(("parallel", "parallel")),
        name="dft2d",
    )(act, twc, tws, mrcat, mccat)
    return y.reshape(batch * seq, width)


def _silu(x):
    h = 0.5 * x
    return h + h * jnp.tanh(h)


def _gate_consts(lb):
    c1 = 0.5 * (1.0 - lb)
    return lb + c1, c1


def _gates(z, c0, c1):
    f = c0 + c1 * jnp.tanh(0.5 * z)
    f_floor = jnp.maximum(f, MIN_FORGET)
    return f_floor, jnp.log(f_floor), 1.0 - f


def _cumsum_rows(x, tri):
    w = x.shape[1]
    hi = x.astype(BF16)
    lo = (x - hi.astype(F32)).astype(BF16)
    y = jnp.dot(tri, jnp.concatenate([hi, lo], axis=1), preferred_element_type=F32)
    return y[:, :w] + y[:, w:]


def _boundary_rows(x, n, row8):
    c, w = x.shape
    parts = []
    if 2 * n >= SUBLANES:
        for p in range(c // (2 * n)):
            r = p * 2 * n + n - 1
            parts.append(jnp.broadcast_to(x[r:r + 1, :], (2 * n, w)))
    else:
        lo = row8 < 4
        for p in range(c // SUBLANES):
            r = p * SUBLANES
            a = jnp.broadcast_to(x[r + 1:r + 2, :], (SUBLANES, w))
            b = jnp.broadcast_to(x[r + 5:r + 6, :], (SUBLANES, w))
            parts.append(jnp.where(lo, a, b))
    return jnp.concatenate(parts, axis=0)


def _nt_dot(a, b):
    return lax.dot_general(a, b, (((1,), (1,)), ((), ())), preferred_element_type=F32)


def _intra_scores(q, kf, kb, ff, fb, b_f, bx_b, b_b, row, row8, xor):
    c = q.shape[0]
    products = []
    n = c // 2
    while n >= 1:
        if n >= SUBLANES:
            qt_parts, kt_parts = [], []
            for lo in range(0, c, 2 * n):
                mid, hi = lo + n, lo + 2 * n
                bf_m, bb_m = b_f[mid - 1:mid, :], b_b[mid - 1:mid, :]
                qt_parts.append(q[lo:mid] * jnp.exp(bb_m - bx_b[lo:mid]))
                kt_parts.append(kf[lo:mid] * jnp.exp(bf_m - b_f[lo:mid]))
                qt_parts.append(q[mid:hi] * jnp.exp(b_f[mid:hi] - bf_m))
                kt_parts.append(kb[mid:hi] * jnp.exp(bx_b[mid:hi] - bb_m))
            qt, kt = jnp.concatenate(qt_parts, axis=0), jnp.concatenate(kt_parts, axis=0)
        else:
            odd = (row & n) != 0
            kk = jnp.where(odd, kb, kf)
            if n == 1:
                qt = q * jnp.where(odd, ff, fb)
                kt = kk
            else:
                d1 = b_f - _boundary_rows(b_f, n, row8)
                d2 = _boundary_rows(b_b, n, row8) - bx_b
                qt = q * jnp.exp(jnp.minimum(d1, d2))
                kt = kk * jnp.exp(-jnp.maximum(d1, d2))
        products.append((n, _nt_dot(qt.astype(BF16), kt.astype(BF16))))
        n //= 2
    return products


def _assemble_scores(products, xor):
    p = None
    for n, pn in products:
        p = pn if p is None else jnp.where(xor < 2 * n, pn, p)
    return jnp.where(xor == 0, 0.0, p)


def _scan_fwd_kernel(q_ref, zf_ref, zb_ref, v_ref, lbf_ref, lbb_ref,
                     o_ref, qb_ref, kb_ref, tot_ref, vt_ref, st_ref, *, chunk):
    @pl.when(pl.program_id(2) == 0)
    def _():
        st_ref[...] = jnp.zeros_like(st_ref)

    c = chunk
    w = q_ref.shape[1]
    ri = lax.broadcasted_iota(jnp.int32, (c, c), 0)
    ci = lax.broadcasted_iota(jnp.int32, (c, c), 1)
    tri = (ri >= ci).astype(BF16)
    xor = ri ^ ci
    row = lax.broadcasted_iota(jnp.int32, (c, w), 0)
    row8 = lax.broadcasted_iota(jnp.int32, (SUBLANES, w), 0)
    c0f, c1f = _gate_consts(lbf_ref[...])
    c0b, c1b = _gate_consts(lbb_ref[...])

    nch = q_ref.shape[0] // c
    work = [dict(sl=pl.ds(ch * c, c), ch=ch) for ch in range(nch)]

    def stage_gates(s):
        sl = s["sl"]
        s["q"] = _silu(q_ref[sl, :])
        s["v"] = v_ref[sl, :]
        s["ff"], lf, s["kf"] = _gates(zf_ref[sl, :], c0f, c1f)
        s["fb"], s["lfb"], s["kb"] = _gates(zb_ref[sl, :], c0b, c1b)
        s["b_f"] = _cumsum_rows(lf, tri)
        s["b_b"] = _cumsum_rows(s["lfb"], tri)

    def stage_levels(s):
        q, v, kf, kb, b_f, b_b = s["q"], s["v"], s["kf"], s["kb"], s["b_f"], s["b_b"]
        bx_b = b_b - s["lfb"]
        s["products"] = _intra_scores(q, kf, kb, s["ff"], s["fb"], b_f, bx_b, b_b, row, row8, xor)
        s["diag"] = jnp.sum(q * (kf + kb), axis=-1, keepdims=True)
        b_last = b_f[c - 1:c, :]
        s["qh"] = (q * jnp.exp(b_f)).astype(BF16)
        kh = (kf * jnp.exp(b_last - b_f)).astype(BF16)
        vt = v.T.astype(BF16)
        vt_ref[:, s["sl"]] = vt
        s["upd"] = jnp.dot(vt, kh, preferred_element_type=F32)
        s["decay"] = jnp.exp(b_last)
        tot = b_b[c - 1:c, :]
        qb_ref[s["sl"], :] = (q * jnp.exp(tot - bx_b)).astype(BF16)
        kb_ref[s["sl"], :] = (kb * jnp.exp(bx_b)).astype(BF16)
        tot_ref[pl.ds(s["ch"] * SUBLANES, SUBLANES), :] = jnp.broadcast_to(jnp.exp(tot),
                                                                            (SUBLANES, w))

    def stage_scores(s):
        p = _assemble_scores(s.pop("products"), xor)
        s["pv"] = jnp.dot(p.astype(BF16), s["v"].astype(BF16), preferred_element_type=F32)

    stages = (stage_gates, stage_levels, stage_scores)
    for step in range(nch + len(stages) - 1):
        for k, stage in enumerate(stages):
            if 0 <= step - k < nch:
                stage(work[step - k])

    st = st_ref[...]
    for s in work:
        o_ref[s["sl"], :] = s["pv"] + s["diag"] * s["v"] + _nt_dot(s["qh"], st.astype(BF16))
        st = st * s["decay"] + s["upd"]
    st_ref[...] = st


def _scan_bwd_kernel(qb_ref, kb_ref, tot_ref, vt_ref, g_ref, o1_ref, hg_ref, o_ref, st_ref, *,
                     chunk, lookahead):
    @pl.when(pl.program_id(2) == 0)
    def _():
        st_ref[...] = jnp.zeros_like(st_ref)

    c = chunk
    hd = HEAD_DIM
    nch = o1_ref.shape[0] // c
    items = [(hh, ch) for hh in range(st_ref.shape[0]) for ch in reversed(range(nch))]
    rows = {ch: pl.ds(ch * c, c) for ch in range(nch)}
    lanes = {hh: pl.ds(hh * hd, hd) for hh in range(st_ref.shape[0])}

    upd = {(hh, ch): jnp.dot(vt_ref[lanes[hh], rows[ch]], kb_ref[rows[ch], lanes[hh]],
                             preferred_element_type=F32) for hh, ch in items}
    st_in = {}
    for hh in range(st_ref.shape[0]):
        st = st_ref[hh]
        for ch in reversed(range(nch)):
            st_in[hh, ch] = st.astype(BF16)
            st = st * tot_ref[pl.ds(ch * SUBLANES, 1), lanes[hh]] + upd[hh, ch]
        st_ref[hh] = st

    inter = {}
    for i in range(len(items) + lookahead):
        if i < len(items):
            hh, ch = items[i]
            inter[hh, ch] = _nt_dot(qb_ref[rows[ch], lanes[hh]], st_in.pop((hh, ch)))
        if i >= lookahead:
            hh, ch = items[i - lookahead]
            o = o1_ref[rows[ch], lanes[hh]] + inter.pop((hh, ch))
            o = o * lax.rsqrt(jnp.mean(o * o, axis=-1, keepdims=True) + NORM_EPS)
            o_ref[rows[ch], lanes[hh]] = (o * hg_ref[:, lanes[hh]]
                                          * _silu(g_ref[rows[ch], lanes[hh]])).astype(BF16)


def _scan_call(mid, lb_f, lb_b, hg_gain, *, row0, batch, seq, heads, cols, tc, tc_bwd,
               heads_per_step_bwd=2):
    hd = HEAD_DIM
    nb = seq // tc
    rb0 = row0 // tc
    cq, czf, czb, cv, cg = (c // hd for c in cols)
    assert heads % heads_per_step_bwd == 0 and cg % heads_per_step_bwd == 0
    rows_per_tot = SCAN_CHUNK // SUBLANES
    hw = heads * hd

    def tok(col):
        return pl.BlockSpec((tc, hd), lambda b, h, j: (rb0 + b * nb + j, col + h))

    def own(rows):
        return pl.BlockSpec((rows, hd), lambda b, h, j: (b * nb + j, h))

    par = pl.BlockSpec((1, hd), lambda b, h, j: (0, h))
    o1, qb, kb, tot, vt = pl.pallas_call(
        functools.partial(_scan_fwd_kernel, chunk=SCAN_CHUNK),
        grid=(batch, heads, nb),
        in_specs=[tok(cq), tok(czf), tok(czb), tok(cv), par, par],
        out_specs=[own(tc), own(tc), own(tc), own(tc // rows_per_tot),
                   pl.BlockSpec((hd, tc), lambda b, h, j: (h, b * nb + j))],
        out_shape=[jax.ShapeDtypeStruct((batch * seq, hw), F32),
                   jax.ShapeDtypeStruct((batch * seq, hw), BF16),
                   jax.ShapeDtypeStruct((batch * seq, hw), BF16),
                   jax.ShapeDtypeStruct((batch * seq // rows_per_tot, hw), F32),
                   jax.ShapeDtypeStruct((hw, batch * seq), BF16)],
        scratch_shapes=[pltpu.VMEM((hd, hd), F32)],
        compiler_params=_cparams(("parallel", "parallel", "arbitrary")),
        name="hgrn2_fwd",
    )(mid, mid, mid, mid, lb_f, lb_b)

    nb2 = seq // tc_bwd
    rb2 = row0 // tc_bwd

    hps = heads_per_step_bwd
    wd = hd * hps

    def own_rev(rows):
        return pl.BlockSpec((rows, wd), lambda b, h, j: (b * nb2 + nb2 - 1 - j, h))

    return pl.pallas_call(
        functools.partial(_scan_bwd_kernel, chunk=SCAN_CHUNK, lookahead=2),
        grid=(batch, heads // hps, nb2),
        in_specs=[own_rev(tc_bwd), own_rev(tc_bwd), own_rev(tc_bwd // rows_per_tot),
                  pl.BlockSpec((wd, tc_bwd), lambda b, h, j: (h, b * nb2 + nb2 - 1 - j)),
                  pl.BlockSpec((tc_bwd, wd),
                               lambda b, h, j: (rb2 + b * nb2 + nb2 - 1 - j, cg // hps + h)),
                  own_rev(tc_bwd), pl.BlockSpec((1, wd), lambda b, h, j: (0, h))],
        out_specs=own_rev(tc_bwd),
        out_shape=jax.ShapeDtypeStruct((batch * seq, hw), BF16),
        scratch_shapes=[pltpu.VMEM((hps, hd, hd), F32)],
        compiler_params=_cparams(("parallel", "parallel", "arbitrary")),
        name="hgrn2_bwd",
    )(qb, kb, tot, vt, mid, o1, hg_gain)


def _trunk(x, groups, params, *, ffn_tm, ffn_tf, proj_tm, proj_tn, out_tm, scan_tc, scan_tc_bwd):
    (ffn1_norm, ffn1_w_in, ffn1_w_down, mix_norm, w_in, lb_logits, hg_norm, w_fourier_out,
     w_hgrn_out, w_out, ffn2_norm, ffn2_w_in, ffn2_w_down, final_norm) = params
    depth, d = mix_norm.shape
    fw = w_fourier_out.shape[1]
    hw = w_hgrn_out.shape[1]
    heads = hw // HEAD_DIM
    d_ff = ffn1_w_down.shape[1]
    cols = tuple(j * hw for j in (0, 1, 2, 3, 4))

    def ffn_half(x, norm, w_in_l, w_down_l):
        return _ffn_call(x, norm[None, :], w_in_l[:, :d_ff].astype(BF16),
                         w_in_l[:, d_ff:].astype(BF16), w_down_l.astype(BF16),
                         tm=ffn_tm, tf=ffn_tf)

    lb = _lower_bound_call(lb_logits)
    dft = {seq: _dft_constants(seq, FOURIER_GROUP_DIM) for (_, _, seq) in groups}

    for l in range(depth):
        x = ffn_half(x, ffn1_norm[l], ffn1_w_in[l], ffn1_w_down[l])

        w_l = w_in[l]
        w_perm = jnp.concatenate([w_l[:, fw:fw + 5 * hw], w_l[:, :fw], w_l[:, fw + 5 * hw:]],
                                 axis=1).astype(BF16)
        mid, gates = _proj_call(x, mix_norm[l][None, :], w_perm, tm=proj_tm, tn=proj_tn,
                                n_gates=2 * d)
        fm_parts, ob_parts = [], []
        for (row0, batch, seq) in groups:
            fm_parts.append(_fourier_call(mid, dft[seq], row0=row0, batch=batch, seq=seq,
                                          col0=5 * hw, width=fw))
            ob_parts.append(_scan_call(mid, lb[0, l][None, :], lb[1, l][None, :],
                                       hg_norm[l][None, :], row0=row0, batch=batch, seq=seq,
                                       heads=heads, cols=cols, tc=scan_tc,
                                       tc_bwd=min(scan_tc_bwd, seq)))
        x = _mixout_call(x, fm_parts, ob_parts, gates, w_fourier_out[l].astype(BF16),
                         w_hgrn_out[l].astype(BF16), w_out[l].astype(BF16), tm=out_tm)

        x = ffn_half(x, ffn2_norm[l], ffn2_w_in[l], ffn2_w_down[l])

    return _norm_call(x, final_norm[None, :], [batch * seq for (_, batch, seq) in groups],
                      tm=out_tm)


def kernel(x_prompt, x_sample, ffn1_norm, ffn1_w_in, ffn1_w_down, mix_norm, w_in, lb_logits, hg_norm, w_fourier_out, w_hgrn_out, w_out, ffn2_norm, ffn2_w_in, ffn2_w_down, final_norm):
    bp, sp, d = x_prompt.shape
    bs, ss, _ = x_sample.shape
    x = jnp.concatenate([x_prompt.reshape(bp * sp, d), x_sample.reshape(bs * ss, d)], axis=0)
    groups = [(0, bp, sp), (bp * sp, bs, ss)]
    params = (ffn1_norm, ffn1_w_in, ffn1_w_down, mix_norm, w_in, lb_logits, hg_norm,
              w_fourier_out, w_hgrn_out, w_out, ffn2_norm, ffn2_w_in, ffn2_w_down, final_norm)
    y_prompt, y_sample = _trunk(x, groups, params, ffn_tm=1024, ffn_tf=512, proj_tm=1024,
                                proj_tn=1024, out_tm=256, scan_tc=1024, scan_tc_bwd=2048)
    return (y_prompt.reshape(bp, sp, d), y_sample.reshape(bs, ss, d))
```

```python
import functools

import ml_dtypes
import numpy as np
import jax
import jax.numpy as jnp
from jax import lax
from jax.experimental import pallas as pl
from jax.experimental.pallas import tpu as pltpu

F32 = jnp.float32
BF16 = jnp.bfloat16

NORM_EPS = 1e-6
MIN_FORGET = 1e-30
LANES = 128
SUBLANES = 8
HEAD_DIM = 128
FOURIER_GROUP_DIM = 128
SCAN_CHUNK = 128
DFT_BLOCK = 128
VMEM_LIMIT = 56 * 1024 * 1024


def _cparams(sem):
    return pltpu.CompilerParams(dimension_semantics=sem, vmem_limit_bytes=VMEM_LIMIT)


def _rms_scale(x, gain):
    ms = jnp.mean(x * x, axis=-1, keepdims=True)
    return x * lax.rsqrt(ms + NORM_EPS) * gain


def _ffn_kernel(x_ref, gain_ref, wg_ref, wu_ref, wd_ref, o_ref, h_ref, *, last_cols):
    j = pl.program_id(1)
    last = pl.num_programs(1) - 1
    tf = wg_ref.shape[1]

    @pl.when(j == 0)
    def _():
        x = x_ref[...]
        h_ref[...] = _rms_scale(x, gain_ref[...]).astype(BF16)
        o_ref[...] = x

    def step(cols):
        h = h_ref[...]
        g = jnp.dot(h, wg_ref[:, :cols], preferred_element_type=F32)
        u = jnp.dot(h, wu_ref[:, :cols], preferred_element_type=F32)
        a = (g * jax.nn.sigmoid(g) * u * 0.5).astype(BF16)
        o_ref[...] += jnp.dot(a, wd_ref[:cols, :], preferred_element_type=F32)

    pl.when(j < last)(lambda: step(tf))
    pl.when(j == last)(lambda: step(last_cols))


def _ffn_call(x, gain, w_gate, w_up, w_down, *, tm, tf):
    t, d = x.shape
    f = w_down.shape[0]
    nf = pl.cdiv(f, tf)
    last_cols = f - (nf - 1) * tf
    assert last_cols % LANES == 0
    return pl.pallas_call(
        functools.partial(_ffn_kernel, last_cols=last_cols),
        grid=(t // tm, nf),
        in_specs=[
            pl.BlockSpec((tm, d), lambda i, j: (i, 0)),
            pl.BlockSpec((1, d), lambda i, j: (0, 0)),
            pl.BlockSpec((d, tf), lambda i, j: (0, j)),
            pl.BlockSpec((d, tf), lambda i, j: (0, j)),
            pl.BlockSpec((tf, d), lambda i, j: (j, 0)),
        ],
        out_specs=pl.BlockSpec((tm, d), lambda i, j: (i, 0)),
        out_shape=jax.ShapeDtypeStruct((t, d), F32),
        scratch_shapes=[pltpu.VMEM((tm, d), BF16)],
        compiler_params=_cparams(("parallel", "arbitrary")),
        name="ffn_half",
    )(x, gain, w_gate, w_up, w_down)


def _proj_kernel(x_ref, gain_ref, wlo_ref, whi_ref, act_ref, gate_ref, h_ref, *, j_gate):
    j = pl.program_id(1)
    half = wlo_ref.shape[1]

    @pl.when(j == 0)
    def _():
        h_ref[...] = _rms_scale(x_ref[...], gain_ref[...]).astype(BF16)

    def tile(o_ref):
        h = h_ref[...]
        o_ref[:, :half] = jnp.dot(h, wlo_ref[...], preferred_element_type=F32)
        o_ref[:, half:] = jnp.dot(h, whi_ref[...], preferred_element_type=F32)

    pl.when(j < j_gate)(lambda: tile(act_ref))
    pl.when(j >= j_gate)(lambda: tile(gate_ref))


def _proj_call(x, gain, w, *, tm, tn, n_fourier, n_gates):
    t, d = x.shape
    n = w.shape[1]
    half = tn // 2
    n_act = n - n_gates
    j_gate, n_steps = n_act // tn, n // tn
    nfb, nmb = n_fourier // half, (n_act - n_fourier) // half
    assert n_act % tn == 0 and n_gates % tn == 0 and n_fourier % half == 0

    def source_block(a):
        return jnp.where(a < nmb, a + nfb, jnp.where(a < nmb + nfb, a - nmb, a))

    return pl.pallas_call(
        functools.partial(_proj_kernel, j_gate=j_gate),
        grid=(t // tm, n_steps),
        in_specs=[
            pl.BlockSpec((tm, d), lambda i, j: (i, 0)),
            pl.BlockSpec((1, d), lambda i, j: (0, 0)),
            pl.BlockSpec((d, half), lambda i, j: (0, source_block(2 * j))),
            pl.BlockSpec((d, half), lambda i, j: (0, source_block(2 * j + 1))),
        ],
        out_specs=[
            pl.BlockSpec((tm, tn), lambda i, j: (i, jnp.minimum(j, j_gate - 1))),
            pl.BlockSpec((tm, tn), lambda i, j: (i, jnp.maximum(j - j_gate, 0))),
        ],
        out_shape=[jax.ShapeDtypeStruct((t, n_act), F32),
                   jax.ShapeDtypeStruct((t, n_gates), F32)],
        scratch_shapes=[pltpu.VMEM((tm, d), BF16)],
        compiler_params=_cparams(("parallel", "arbitrary")),
        name="mixer_proj",
    )(x, gain, w, w)


def _sigmoid(x):
    return 0.5 + 0.5 * jnp.tanh(0.5 * x)


def _mixout_kernel(*refs, tile_starts):
    ng = len(tile_starts)
    x_ref = refs[0]
    fm_refs = refs[1:1 + ng]
    ob_refs = refs[1 + ng:1 + 2 * ng]
    ga_ref, gb_ref, wfo_ref, who_ref, wo_ref, o_ref = refs[1 + 2 * ng:]
    i = pl.program_id(0)
    fm, ob = fm_refs[0][...], ob_refs[0][...]
    for g in range(1, ng):
        in_g = i >= tile_starts[g]
        fm = jnp.where(in_g, fm_refs[g][...], fm)
        ob = jnp.where(in_g, ob_refs[g][...], ob)
    a = jnp.dot(fm.astype(BF16), wfo_ref[...], preferred_element_type=F32)
    b = jnp.dot(ob, who_ref[...], preferred_element_type=F32)
    m = (_sigmoid(ga_ref[...]) * a + _sigmoid(gb_ref[...]) * b).astype(BF16)
    o_ref[...] = x_ref[...] + jnp.dot(m, wo_ref[...], preferred_element_type=F32)


def _mixout_call(x, fm_parts, ob_parts, gates, w_fo, w_ho, w_out, *, tm):
    t, d = x.shape
    tile_starts, tile_counts, start = [], [], 0
    for part in fm_parts:
        tile_starts.append(start)
        tile_counts.append(part.shape[0] // tm)
        start += part.shape[0] // tm

    def group_spec(part, g):
        return pl.BlockSpec(
            (tm, part.shape[1]),
            lambda i: (jnp.clip(i - tile_starts[g], 0, tile_counts[g] - 1), 0))

    def resident(w):
        return pl.BlockSpec(w.shape, lambda i: (0, 0), pipeline_mode=pl.Buffered(1))

    return pl.pallas_call(
        functools.partial(_mixout_kernel, tile_starts=tuple(tile_starts)),
        grid=(t // tm,),
        in_specs=(
            [pl.BlockSpec((tm, d), lambda i: (i, 0))]
            + [group_spec(p, g) for g, p in enumerate(fm_parts)]
            + [group_spec(p, g) for g, p in enumerate(ob_parts)]
            + [pl.BlockSpec((tm, d), lambda i: (i, 0)), pl.BlockSpec((tm, d), lambda i: (i, 1)),
               resident(w_fo), resident(w_ho), resident(w_out)]),
        out_specs=pl.BlockSpec((tm, d), lambda i: (i, 0)),
        out_shape=jax.ShapeDtypeStruct((t, d), F32),
        compiler_params=_cparams(("parallel",)),
        name="mixer_out",
    )(x, *fm_parts, *ob_parts, gates, gates, w_fo, w_ho, w_out)


def _norm_kernel(x_ref, gain_ref, *o_refs, tile_starts):
    i = pl.program_id(0)
    bounds = tuple(tile_starts) + (pl.num_programs(0),)
    for g, o_ref in enumerate(o_refs):
        @pl.when((i >= bounds[g]) & (i < bounds[g + 1]))
        def _():
            o_ref[...] = _rms_scale(x_ref[...], gain_ref[...])


def _norm_call(x, gain, group_rows, *, tm):
    t, d = x.shape
    tile_starts, tile_counts, start = [], [], 0
    for rows in group_rows:
        tile_starts.append(start)
        tile_counts.append(rows // tm)
        start += rows // tm

    def group_spec(g):
        return pl.BlockSpec(
            (tm, d), lambda i: (jnp.clip(i - tile_starts[g], 0, tile_counts[g] - 1), 0))

    return pl.pallas_call(
        functools.partial(_norm_kernel, tile_starts=tuple(tile_starts)),
        grid=(t // tm,),
        in_specs=[pl.BlockSpec((tm, d), lambda i: (i, 0)), pl.BlockSpec((1, d), lambda i: (0, 0))],
        out_specs=[group_spec(g) for g in range(len(group_rows))],
        out_shape=[jax.ShapeDtypeStruct((rows, d), F32) for rows in group_rows],
        compiler_params=_cparams(("arbitrary",)),
        name="final_norm",
    )(x, gain)


def _lower_bound_kernel(logit_ref, o_ref, *, depth):
    for d0 in range(0, logit_ref.shape[0], depth):
        rows = [logit_ref[d0 + l:d0 + l + 1, :] for l in range(depth)]
        mx = functools.reduce(jnp.maximum, rows)
        es = [jnp.exp(r - mx) for r in rows]
        tot = functools.reduce(lambda a, b: a + b, es)
        ps = [e / tot for e in es]
        acc = ps[0]
        for l in range(depth):
            if l:
                acc = acc + ps[l]
            o_ref[d0 + l:d0 + l + 1, :] = jnp.maximum(acc - ps[0], 0.0)


def _lower_bound_call(lb_logits):
    dirs, depth, hw = lb_logits.shape
    lb = pl.pallas_call(
        functools.partial(_lower_bound_kernel, depth=depth),
        out_shape=jax.ShapeDtypeStruct((dirs * depth, hw), F32),
        name="lower_bounds",
    )(lb_logits.astype(F32).reshape(dirs * depth, hw))
    return lb.reshape(dirs, depth, hw)


def _split2_np(m):
    hi = m.astype(ml_dtypes.bfloat16).astype(np.float64)
    lo = m - hi
    return hi, lo


def _split2(x):
    hi = x.astype(BF16)
    lo = (x - hi.astype(F32)).astype(BF16)
    return hi, lo


def _dft_constants(s, gd):
    nr = DFT_BLOCK
    kr = np.arange(nr, dtype=np.float64)
    cr = np.cos(2 * np.pi * np.outer(kr, kr) / nr)
    sr = np.sin(2 * np.pi * np.outer(kr, kr) / nr)
    mr = np.block([[cr, sr], [-sr, cr]])
    mrh, mrl = _split2_np(mr)
    mrcat = np.concatenate([mrh, mrh, mrl], axis=1)

    kc = np.arange(gd, dtype=np.float64)
    scale = 1.0 / np.sqrt(float(s) * gd)
    cc = np.cos(2 * np.pi * np.outer(kc, kc) / gd) * scale
    sc = np.sin(2 * np.pi * np.outer(kc, kc) / gd) * scale
    mc = np.concatenate([cc, sc], axis=0)
    mch, mcl = _split2_np(mc)
    mccat = np.concatenate([mch, mch, mcl], axis=0)

    k2 = np.arange(s // nr, dtype=np.float64)
    ang = (2 * np.pi * np.outer(k2, kr) / s).reshape(s, 1)
    twc = jnp.broadcast_to(jnp.asarray(np.cos(ang), F32), (s, LANES))
    tws = jnp.broadcast_to(jnp.asarray(np.sin(ang), F32), (s, LANES))
    return jnp.asarray(mrcat, BF16), jnp.asarray(mccat, BF16), twc, tws


def _c_add(a, b):
    return b if a is None else a if b is None else a + b


def _c_sub(a, b):
    return a if b is None else -b if a is None else a - b


def _fft_blocks(xs):
    n = len(xs)
    if n == 1:
        return xs
    even, odd = _fft_blocks(xs[0::2]), _fft_blocks(xs[1::2])
    out = [None] * n
    for k in range(n // 2):
        o_r, o_i = odd[k]
        if k == 0:
            t_r, t_i = o_r, o_i
        elif 4 * k == n:
            t_r, t_i = o_i, (None if o_r is None else -o_r)
        else:
            wr, wi = float(np.cos(2 * np.pi * k / n)), float(np.sin(2 * np.pi * k / n))
            t_r = _c_add(None if o_r is None else wr * o_r, None if o_i is None else wi * o_i)
            t_i = _c_sub(None if o_i is None else wr * o_i, None if o_r is None else wi * o_r)
        e_r, e_i = even[k]
        out[k] = (_c_add(e_r, t_r), _c_add(e_i, t_i))
        out[k + n // 2] = (_c_sub(e_r, t_r), _c_sub(e_i, t_i))
    return out


def _dft_kernel(x_ref, twc_ref, tws_ref, mr_ref, mc_ref, o_ref):
    nr = DFT_BLOCK
    nb = x_ref.shape[0] // nr

    def rows(ref, j):
        return ref[pl.ds(j * nr, nr), :]

    u = _fft_blocks([(rows(x_ref, j), None) for j in range(nb)])
    for k2 in range(nb):
        u_r, u_i = u[k2]
        u_r = jnp.zeros((nr, x_ref.shape[1]), F32) if u_r is None else u_r
        u_i = jnp.zeros_like(u_r) if u_i is None else u_i
        if k2 == 0:
            t_r, t_i = u_r, u_i
        else:
            c, s = rows(twc_ref, k2), rows(tws_ref, k2)
            t_r, t_i = u_r * c + u_i * s, u_i * c - u_r * s
        th, tl = _split2(jnp.concatenate([t_r, t_i], axis=0))
        v = jnp.dot(mr_ref[...], jnp.concatenate([th, tl, th], axis=0),
                    preferred_element_type=F32)
        vh, vl = _split2(jnp.concatenate([v[:nr], v[nr:]], axis=1))
        o_ref[0, :, k2, :] = jnp.dot(jnp.concatenate([vh, vl, vh], axis=1), mc_ref[...],
                                     preferred_element_type=F32)


def _fourier_call(act, consts, *, row0, batch, seq, col0, width):
    gd = FOURIER_GROUP_DIM
    nb = seq // DFT_BLOCK
    mrcat, mccat, twc, tws = consts

    def resident(a):
        return pl.BlockSpec(a.shape, lambda b, g: (0, 0), pipeline_mode=pl.Buffered(1))

    y = pl.pallas_call(
        _dft_kernel,
        grid=(batch, width // gd),
        in_specs=[pl.BlockSpec((seq, gd), lambda b, g: (row0 // seq + b, col0 // gd + g)),
                  resident(twc), resident(tws), resident(mrcat), resident(mccat)],
        out_specs=pl.BlockSpec((1, DFT_BLOCK, nb, gd), lambda b, g: (b, 0, 0, g)),
        out_shape=jax.ShapeDtypeStruct((batch, DFT_BLOCK, nb, width), F32),
        compiler_params=_cparams(("parallel", "parallel")),
        name="dft2d",
    )(act, twc, tws, mrcat, mccat)
    return y.reshape(batch * seq, width)


def _silu(x):
    h = 0.5 * x
    return h + h * jnp.tanh(h)


def _gate_consts(lb):
    c1 = 0.5 * (1.0 - lb)
    return lb + c1, c1


def _gates(z, c0, c1):
    f = c0 + c1 * jnp.tanh(0.5 * z)
    f_floor = jnp.maximum(f, MIN_FORGET)
    return f_floor, jnp.log(f_floor), 1.0 - f


def _cumsum_rows(x, tri):
    w = x.shape[1]
    hi = x.astype(BF16)
    lo = (x - hi.astype(F32)).astype(BF16)
    y = jnp.dot(tri, jnp.concatenate([hi, lo], axis=1), preferred_element_type=F32)
    return y[:, :w] + y[:, w:]


def _boundary_rows(x, n, row8):
    c, w = x.shape
    parts = []
    if 2 * n >= SUBLANES:
        for p in range(c // (2 * n)):
            r = p * 2 * n + n - 1
            parts.append(jnp.broadcast_to(x[r:r + 1, :], (2 * n, w)))
    else:
        lo = row8 < 4
        for p in range(c // SUBLANES):
            r = p * SUBLANES
            a = jnp.broadcast_to(x[r + 1:r + 2, :], (SUBLANES, w))
            b = jnp.broadcast_to(x[r + 5:r + 6, :], (SUBLANES, w))
            parts.append(jnp.where(lo, a, b))
    return jnp.concatenate(parts, axis=0)


def _nt_dot(a, b):
    return lax.dot_general(a, b, (((1,), (1,)), ((), ())), preferred_element_type=F32)


def _intra_scores(q, kf, kb, ff, fb, b_f, bx_b, b_b, row, row8, xor):
    c = q.shape[0]
    products = []
    n = c // 2
    while n >= 1:
        if n >= SUBLANES:
            qt_parts, kt_parts = [], []
            for lo in range(0, c, 2 * n):
                mid, hi = lo + n, lo + 2 * n
                bf_m, bb_m = b_f[mid - 1:mid, :], b_b[mid - 1:mid, :]
                qt_parts.append(q[lo:mid] * jnp.exp(bb_m - bx_b[lo:mid]))
                kt_parts.append(kf[lo:mid] * jnp.exp(bf_m - b_f[lo:mid]))
                qt_parts.append(q[mid:hi] * jnp.exp(b_f[mid:hi] - bf_m))
                kt_parts.append(kb[mid:hi] * jnp.exp(bx_b[mid:hi] - bb_m))
            qt, kt = jnp.concatenate(qt_parts, axis=0), jnp.concatenate(kt_parts, axis=0)
        else:
            odd = (row & n) != 0
            kk = jnp.where(odd, kb, kf)
            if n == 1:
                qt = q * jnp.where(odd, ff, fb)
                kt = kk
            else:
                d1 = b_f - _boundary_rows(b_f, n, row8)
                d2 = _boundary_rows(b_b, n, row8) - bx_b
                qt = q * jnp.exp(jnp.minimum(d1, d2))
                kt = kk * jnp.exp(-jnp.maximum(d1, d2))
        products.append((n, _nt_dot(qt.astype(BF16), kt.astype(BF16))))
        n //= 2
    return products


def _assemble_scores(products, xor):
    p = None
    for n, pn in products:
        p = pn if p is None else jnp.where(xor < 2 * n, pn, p)
    return jnp.where(xor == 0, 0.0, p)


def _scan_fwd_kernel(q_ref, zf_ref, zb_ref, v_ref, lbf_ref, lbb_ref,
                     o_ref, qb_ref, kb_ref, tot_ref, vt_ref, st_ref, *, chunk):
    @pl.when(pl.program_id(2) == 0)
    def _():
        st_ref[...] = jnp.zeros_like(st_ref)

    c = chunk
    w = q_ref.shape[1]
    ri = lax.broadcasted_iota(jnp.int32, (c, c), 0)
    ci = lax.broadcasted_iota(jnp.int32, (c, c), 1)
    tri = (ri >= ci).astype(BF16)
    xor = ri ^ ci
    row = lax.broadcasted_iota(jnp.int32, (c, w), 0)
    row8 = lax.broadcasted_iota(jnp.int32, (SUBLANES, w), 0)
    c0f, c1f = _gate_consts(lbf_ref[...])
    c0b, c1b = _gate_consts(lbb_ref[...])

    nch = q_ref.shape[0] // c
    work = [dict(sl=pl.ds(ch * c, c), ch=ch) for ch in range(nch)]

    def stage_gates(s):
        sl = s["sl"]
        s["q"] = _silu(q_ref[sl, :])
        s["v"] = v_ref[sl, :]
        s["ff"], lf, s["kf"] = _gates(zf_ref[sl, :], c0f, c1f)
        s["fb"], s["lfb"], s["kb"] = _gates(zb_ref[sl, :], c0b, c1b)
        s["b_f"] = _cumsum_rows(lf, tri)
        s["b_b"] = _cumsum_rows(s["lfb"], tri)

    def stage_levels(s):
        q, v, kf, kb, b_f, b_b = s["q"], s["v"], s["kf"], s["kb"], s["b_f"], s["b_b"]
        bx_b = b_b - s["lfb"]
        s["products"] = _intra_scores(q, kf, kb, s["ff"], s["fb"], b_f, bx_b, b_b, row, row8, xor)
        s["diag"] = jnp.sum(q * (kf + kb), axis=-1, keepdims=True)
        b_last = b_f[c - 1:c, :]
        s["qh"] = (q * jnp.exp(b_f)).astype(BF16)
        kh = (kf * jnp.exp(b_last - b_f)).astype(BF16)
        vt = v.T.astype(BF16)
        vt_ref[:, s["sl"]] = vt
        s["upd"] = jnp.dot(vt, kh, preferred_element_type=F32)
        s["decay"] = jnp.exp(b_last)
        tot = b_b[c - 1:c, :]
        qb_ref[s["sl"], :] = (q * jnp.exp(tot - bx_b)).astype(BF16)
        kb_ref[s["sl"], :] = (kb * jnp.exp(bx_b)).astype(BF16)
        tot_ref[pl.ds(s["ch"] * SUBLANES, SUBLANES), :] = jnp.broadcast_to(jnp.exp(tot),
                                                                            (SUBLANES, w))

    def stage_scores(s):
        p = _assemble_scores(s.pop("products"), xor)
        s["pv"] = jnp.dot(p.astype(BF16), s["v"].astype(BF16), preferred_element_type=F32)

    stages = (stage_gates, stage_levels, stage_scores)
    for step in range(nch + len(stages) - 1):
        for k, stage in enumerate(stages):
            if 0 <= step - k < nch:
                stage(work[step - k])

    st = st_ref[...]
    for s in work:
        o_ref[s["sl"], :] = s["pv"] + s["diag"] * s["v"] + _nt_dot(s["qh"], st.astype(BF16))
        st = st * s["decay"] + s["upd"]
    st_ref[...] = st


def _scan_bwd_kernel(qb_ref, kb_ref, tot_ref, vt_ref, g_ref, o1_ref, hg_ref, o_ref, st_ref, *,
                     chunk, lookahead):
    @pl.when(pl.program_id(2) == 0)
    def _():
        st_ref[...] = jnp.zeros_like(st_ref)

    c = chunk
    hd = HEAD_DIM
    nch = o1_ref.shape[0] // c
    items = [(hh, ch) for hh in range(st_ref.shape[0]) for ch in reversed(range(nch))]
    rows = {ch: pl.ds(ch * c, c) for ch in range(nch)}
    lanes = {hh: pl.ds(hh * hd, hd) for hh in range(st_ref.shape[0])}

    upd = {(hh, ch): jnp.dot(vt_ref[lanes[hh], rows[ch]], kb_ref[rows[ch], lanes[hh]],
                             preferred_element_type=F32) for hh, ch in items}
    st_in = {}
    for hh in range(st_ref.shape[0]):
        st = st_ref[hh]
        for ch in reversed(range(nch)):
            st_in[hh, ch] = st.astype(BF16)
            st = st * tot_ref[pl.ds(ch * SUBLANES, 1), lanes[hh]] + upd[hh, ch]
        st_ref[hh] = st

    inter = {}
    for i in range(len(items) + lookahead):
        if i < len(items):
            hh, ch = items[i]
            inter[hh, ch] = _nt_dot(qb_ref[rows[ch], lanes[hh]], st_in.pop((hh, ch)))
        if i >= lookahead:
            hh, ch = items[i - lookahead]
            o = o1_ref[rows[ch], lanes[hh]] + inter.pop((hh, ch))
            o = o * lax.rsqrt(jnp.mean(o * o, axis=-1, keepdims=True) + NORM_EPS)
            o_ref[rows[ch], lanes[hh]] = (o * hg_ref[:, lanes[hh]]
                                          * _silu(g_ref[rows[ch], lanes[hh]])).astype(BF16)


def _scan_call(mid, lb_f, lb_b, hg_gain, *, row0, batch, seq, heads, cols, tc, tc_bwd,
               heads_per_step_bwd=2):
    hd = HEAD_DIM
    nb = seq // tc
    rb0 = row0 // tc
    cq, czf, czb, cv, cg = (c // hd for c in cols)
    assert heads % heads_per_step_bwd == 0 and cg % heads_per_step_bwd == 0
    rows_per_tot = SCAN_CHUNK // SUBLANES
    hw = heads * hd

    def tok(col):
        return pl.BlockSpec((tc, hd), lambda b, h, j: (rb0 + b * nb + j, col + h))

    def own(rows):
        return pl.BlockSpec((rows, hd), lambda b, h, j: (b * nb + j, h))

    par = pl.BlockSpec((1, hd), lambda b, h, j: (0, h))
    o1, qb, kb, tot, vt = pl.pallas_call(
        functools.partial(_scan_fwd_kernel, chunk=SCAN_CHUNK),
        grid=(batch, heads, nb),
        in_specs=[tok(cq), tok(czf), tok(czb), tok(cv), par, par],
        out_specs=[own(tc), own(tc), own(tc), own(tc // rows_per_tot),
                   pl.BlockSpec((hd, tc), lambda b, h, j: (h, b * nb + j))],
        out_shape=[jax.ShapeDtypeStruct((batch * seq, hw), F32),
                   jax.ShapeDtypeStruct((batch * seq, hw), BF16),
                   jax.ShapeDtypeStruct((batch * seq, hw), BF16),
                   jax.ShapeDtypeStruct((batch * seq // rows_per_tot, hw), F32),
                   jax.ShapeDtypeStruct((hw, batch * seq), BF16)],
        scratch_shapes=[pltpu.VMEM((hd, hd), F32)],
        compiler_params=_cparams(("parallel", "parallel", "arbitrary")),
        name="hgrn2_fwd",
    )(mid, mid, mid, mid, lb_f, lb_b)

    nb2 = seq // tc_bwd
    rb2 = row0 // tc_bwd

    hps = heads_per_step_bwd
    wd = hd * hps

    def own_rev(rows):
        return pl.BlockSpec((rows, wd), lambda b, h, j: (b * nb2 + nb2 - 1 - j, h))

    return pl.pallas_call(
        functools.partial(_scan_bwd_kernel, chunk=SCAN_CHUNK, lookahead=2),
        grid=(batch, heads // hps, nb2),
        in_specs=[own_rev(tc_bwd), own_rev(tc_bwd), own_rev(tc_bwd // rows_per_tot),
                  pl.BlockSpec((wd, tc_bwd), lambda b, h, j: (h, b * nb2 + nb2 - 1 - j)),
                  pl.BlockSpec((tc_bwd, wd),
                               lambda b, h, j: (rb2 + b * nb2 + nb2 - 1 - j, cg // hps + h)),
                  own_rev(tc_bwd), pl.BlockSpec((1, wd), lambda b, h, j: (0, h))],
        out_specs=own_rev(tc_bwd),
        out_shape=jax.ShapeDtypeStruct((batch * seq, hw), BF16),
        scratch_shapes=[pltpu.VMEM((hps, hd, hd), F32)],
        compiler_params=_cparams(("parallel", "parallel", "arbitrary")),
        name="hgrn2_bwd",
    )(qb, kb, tot, vt, mid, o1, hg_gain)


def _trunk(x, groups, params, *, ffn_tm, ffn_tf, proj_tm, proj_tn, out_tm, scan_tc, scan_tc_bwd):
    (ffn1_norm, ffn1_w_in, ffn1_w_down, mix_norm, w_in, lb_logits, hg_norm, w_fourier_out,
     w_hgrn_out, w_out, ffn2_norm, ffn2_w_in, ffn2_w_down, final_norm) = params
    depth, d = mix_norm.shape
    fw = w_fourier_out.shape[1]
    hw = w_hgrn_out.shape[1]
    heads = hw // HEAD_DIM
    d_ff = ffn1_w_down.shape[1]
    cols = tuple(j * hw for j in (0, 1, 2, 3, 4))

    def ffn_half(x, norm, w_in_l, w_down_l):
        return _ffn_call(x, norm[None, :], w_in_l[:, :d_ff].astype(BF16),
                         w_in_l[:, d_ff:].astype(BF16), w_down_l.astype(BF16),
                         tm=ffn_tm, tf=ffn_tf)

    lb = _lower_bound_call(lb_logits)
    dft = {seq: _dft_constants(seq, FOURIER_GROUP_DIM) for (_, _, seq) in groups}

    for l in range(depth):
        x = ffn_half(x, ffn1_norm[l], ffn1_w_in[l], ffn1_w_down[l])

        mid, gates = _proj_call(x, mix_norm[l][None, :], w_in[l].astype(BF16), tm=proj_tm,
                                tn=proj_tn, n_fourier=fw, n_gates=2 * d)
        fm_parts, ob_parts = [], []
        for (row0, batch, seq) in groups:
            fm_parts.append(_fourier_call(mid, dft[seq], row0=row0, batch=batch, seq=seq,
                                          col0=5 * hw, width=fw))
            ob_parts.append(_scan_call(mid, lb[0, l][None, :], lb[1, l][None, :],
                                       hg_norm[l][None, :], row0=row0, batch=batch, seq=seq,
                                       heads=heads, cols=cols, tc=scan_tc,
                                       tc_bwd=min(scan_tc_bwd, seq)))
        x = _mixout_call(x, fm_parts, ob_parts, gates, w_fourier_out[l].astype(BF16),
                         w_hgrn_out[l].astype(BF16), w_out[l].astype(BF16), tm=out_tm)

        x = ffn_half(x, ffn2_norm[l], ffn2_w_in[l], ffn2_w_down[l])

    return _norm_call(x, final_norm[None, :], [batch * seq for (_, batch, seq) in groups],
                      tm=out_tm)


def kernel(x_prompt, x_sample, ffn1_norm, ffn1_w_in, ffn1_w_down, mix_norm, w_in, lb_logits, hg_norm, w_fourier_out, w_hgrn_out, w_out, ffn2_norm, ffn2_w_in, ffn2_w_down, final_norm):
    bp, sp, d = x_prompt.shape
    bs, ss, _ = x_sample.shape
    x = jnp.concatenate([x_prompt.reshape(bp * sp, d), x_sample.reshape(bs * ss, d)], axis=0)
    groups = [(0, bp, sp), (bp * sp, bs, ss)]
    params = (ffn1_norm, ffn1_w_in, ffn1_w_down, mix_norm, w_in, lb_logits, hg_norm,
              w_fourier_out, w_hgrn_out, w_out, ffn2_norm, ffn2_w_in, ffn2_w_down, final_norm)
    y_prompt, y_sample = _trunk(x, groups, params, ffn_tm=1024, ffn_tf=512, proj_tm=1024,
                                proj_tn=1024, out_tm=256, scan_tc=1024, scan_tc_bwd=2048)
    return (y_prompt.reshape(bp, sp, d), y_sample.reshape(bs, ss, d))
```

```python
import functools

import ml_dtypes
import numpy as np
import jax
import jax.numpy as jnp
from jax import lax
from jax.experimental import pallas as pl
from jax.experimental.pallas import tpu as pltpu

F32 = jnp.float32
BF16 = jnp.bfloat16

NORM_EPS = 1e-6
MIN_FORGET = 1e-30
LANES = 128
SUBLANES = 8
HEAD_DIM = 128
FOURIER_GROUP_DIM = 128
SCAN_CHUNK = 128
DFT_BLOCK = 128
VMEM_LIMIT = 56 * 1024 * 1024


def _cparams(sem):
    return pltpu.CompilerParams(dimension_semantics=sem, vmem_limit_bytes=VMEM_LIMIT)


def _rms_scale(x, gain):
    ms = jnp.mean(x * x, axis=-1, keepdims=True)
    return x * lax.rsqrt(ms + NORM_EPS) * gain


def _ffn_kernel(x_ref, gain_ref, wg_ref, wu_ref, wd_ref, o_ref, h_ref, *, last_cols):
    j = pl.program_id(1)
    last = pl.num_programs(1) - 1
    tf = wg_ref.shape[1]

    @pl.when(j == 0)
    def _():
        x = x_ref[...]
        h_ref[...] = _rms_scale(x, gain_ref[...]).astype(BF16)
        o_ref[...] = x

    def step(cols):
        h = h_ref[...]
        g = jnp.dot(h, wg_ref[:, :cols], preferred_element_type=F32)
        u = jnp.dot(h, wu_ref[:, :cols], preferred_element_type=F32)
        a = (g * jax.nn.sigmoid(g) * u * 0.5).astype(BF16)
        o_ref[...] += jnp.dot(a, wd_ref[:cols, :], preferred_element_type=F32)

    pl.when(j < last)(lambda: step(tf))
    pl.when(j == last)(lambda: step(last_cols))


def _ffn_call(x, gain, w_gate, w_up, w_down, *, tm, tf):
    t, d = x.shape
    f = w_down.shape[0]
    nf = pl.cdiv(f, tf)
    last_cols = f - (nf - 1) * tf
    assert last_cols % LANES == 0
    return pl.pallas_call(
        functools.partial(_ffn_kernel, last_cols=last_cols),
        grid=(t // tm, nf),
        in_specs=[
            pl.BlockSpec((tm, d), lambda i, j: (i, 0)),
            pl.BlockSpec((1, d), lambda i, j: (0, 0)),
            pl.BlockSpec((d, tf), lambda i, j: (0, j)),
            pl.BlockSpec((d, tf), lambda i, j: (0, j)),
            pl.BlockSpec((tf, d), lambda i, j: (j, 0)),
        ],
        out_specs=pl.BlockSpec((tm, d), lambda i, j: (i, 0)),
        out_shape=jax.ShapeDtypeStruct((t, d), F32),
        scratch_shapes=[pltpu.VMEM((tm, d), BF16)],
        compiler_params=_cparams(("parallel", "arbitrary")),
        name="ffn_half",
    )(x, gain, w_gate, w_up, w_down)


def _proj_kernel(x_ref, gain_ref, wlo_ref, whi_ref, act_ref, gate_ref, h_ref, *, j_gate):
    j = pl.program_id(1)
    half = wlo_ref.shape[1]

    @pl.when(j == 0)
    def _():
        h_ref[...] = _rms_scale(x_ref[...], gain_ref[...]).astype(BF16)

    def tile(o_ref):
        h = h_ref[...]
        o_ref[:, :half] = jnp.dot(h, wlo_ref[...], preferred_element_type=F32)
        o_ref[:, half:] = jnp.dot(h, whi_ref[...], preferred_element_type=F32)

    pl.when(j < j_gate)(lambda: tile(act_ref))
    pl.when(j >= j_gate)(lambda: tile(gate_ref))


def _proj_call(x, gain, w, *, tm, tn, n_fourier, n_gates):
    t, d = x.shape
    n = w.shape[1]
    half = tn // 2
    n_act = n - n_gates
    j_gate, n_steps = n_act // tn, n // tn
    nfb, nmb = n_fourier // half, (n_act - n_fourier) // half
    assert n_act % tn == 0 and n_gates % tn == 0 and n_fourier % half == 0

    def source_block(a):
        return jnp.where(a < nmb, a + nfb, jnp.where(a < nmb + nfb, a - nmb, a))

    return pl.pallas_call(
        functools.partial(_proj_kernel, j_gate=j_gate),
        grid=(t // tm, n_steps),
        in_specs=[
            pl.BlockSpec((tm, d), lambda i, j: (i, 0)),
            pl.BlockSpec((1, d), lambda i, j: (0, 0)),
            pl.BlockSpec((d, half), lambda i, j: (0, source_block(2 * j))),
            pl.BlockSpec((d, half), lambda i, j: (0, source_block(2 * j + 1))),
        ],
        out_specs=[
            pl.BlockSpec((tm, tn), lambda i, j: (i, jnp.minimum(j, j_gate - 1))),
            pl.BlockSpec((tm, tn), lambda i, j: (i, jnp.maximum(j - j_gate, 0))),
        ],
        out_shape=[jax.ShapeDtypeStruct((t, n_act), F32),
                   jax.ShapeDtypeStruct((t, n_gates), F32)],
        scratch_shapes=[pltpu.VMEM((tm, d), BF16)],
        compiler_params=_cparams(("parallel", "arbitrary")),
        name="mixer_proj",
    )(x, gain, w, w)


def _sigmoid(x):
    return 0.5 + 0.5 * jnp.tanh(0.5 * x)


def _mixout_kernel(*refs, tile_starts):
    ng = len(tile_starts)
    x_ref = refs[0]
    fm_refs = refs[1:1 + ng]
    ob_refs = refs[1 + ng:1 + 2 * ng]
    ga_ref, gb_ref, wfo_ref, who_ref, wo_ref, o_ref = refs[1 + 2 * ng:]
    i = pl.program_id(0)
    fm, ob = fm_refs[0][...], ob_refs[0][...]
    for g in range(1, ng):
        in_g = i >= tile_starts[g]
        fm = jnp.where(in_g, fm_refs[g][...], fm)
        ob = jnp.where(in_g, ob_refs[g][...], ob)
    a = jnp.dot(fm.astype(BF16), wfo_ref[...], preferred_element_type=F32)
    b = jnp.dot(ob, who_ref[...], preferred_element_type=F32)
    m = (_sigmoid(ga_ref[...]) * a + _sigmoid(gb_ref[...]) * b).astype(BF16)
    o_ref[...] = x_ref[...] + jnp.dot(m, wo_ref[...], preferred_element_type=F32)


def _mixout_call(x, fm_parts, ob_parts, gates, w_fo, w_ho, w_out, *, tm):
    t, d = x.shape
    tile_starts, tile_counts, start = [], [], 0
    for part in fm_parts:
        tile_starts.append(start)
        tile_counts.append(part.shape[0] // tm)
        start += part.shape[0] // tm

    def group_spec(part, g):
        return pl.BlockSpec(
            (tm, part.shape[1]),
            lambda i: (jnp.clip(i - tile_starts[g], 0, tile_counts[g] - 1), 0))

    def resident(w):
        return pl.BlockSpec(w.shape, lambda i: (0, 0), pipeline_mode=pl.Buffered(1))

    return pl.pallas_call(
        functools.partial(_mixout_kernel, tile_starts=tuple(tile_starts)),
        grid=(t // tm,),
        in_specs=(
            [pl.BlockSpec((tm, d), lambda i: (i, 0))]
            + [group_spec(p, g) for g, p in enumerate(fm_parts)]
            + [group_spec(p, g) for g, p in enumerate(ob_parts)]
            + [pl.BlockSpec((tm, d), lambda i: (i, 0)), pl.BlockSpec((tm, d), lambda i: (i, 1)),
               resident(w_fo), resident(w_ho), resident(w_out)]),
        out_specs=pl.BlockSpec((tm, d), lambda i: (i, 0)),
        out_shape=jax.ShapeDtypeStruct((t, d), F32),
        compiler_params=_cparams(("parallel",)),
        name="mixer_out",
    )(x, *fm_parts, *ob_parts, gates, gates, w_fo, w_ho, w_out)


def _norm_kernel(x_ref, gain_ref, *o_refs, tile_starts):
    i = pl.program_id(0)
    bounds = tuple(tile_starts) + (pl.num_programs(0),)
    for g, o_ref in enumerate(o_refs):
        @pl.when((i >= bounds[g]) & (i < bounds[g + 1]))
        def _():
            o_ref[...] = _rms_scale(x_ref[...], gain_ref[...])


def _norm_call(x, gain, group_rows, *, tm):
    t, d = x.shape
    tile_starts, tile_counts, start = [], [], 0
    for rows in group_rows:
        tile_starts.append(start)
        tile_counts.append(rows // tm)
        start += rows // tm

    def group_spec(g):
        return pl.BlockSpec(
            (tm, d), lambda i: (jnp.clip(i - tile_starts[g], 0, tile_counts[g] - 1), 0))

    return pl.pallas_call(
        functools.partial(_norm_kernel, tile_starts=tuple(tile_starts)),
        grid=(t // tm,),
        in_specs=[pl.BlockSpec((tm, d), lambda i: (i, 0)), pl.BlockSpec((1, d), lambda i: (0, 0))],
        out_specs=[group_spec(g) for g in range(len(group_rows))],
        out_shape=[jax.ShapeDtypeStruct((rows, d), F32) for rows in group_rows],
        compiler_params=_cparams(("arbitrary",)),
        name="final_norm",
    )(x, gain)


def _lower_bound_kernel(logit_ref, o_ref, *, depth):
    for d0 in range(0, logit_ref.shape[0], depth):
        rows = [logit_ref[d0 + l:d0 + l + 1, :] for l in range(depth)]
        mx = functools.reduce(jnp.maximum, rows)
        es = [jnp.exp(r - mx) for r in rows]
        tot = functools.reduce(lambda a, b: a + b, es)
        ps = [e / tot for e in es]
        acc = ps[0]
        for l in range(depth):
            if l:
                acc = acc + ps[l]
            o_ref[d0 + l:d0 + l + 1, :] = jnp.maximum(acc - ps[0], 0.0)


def _lower_bound_call(lb_logits):
    dirs, depth, hw = lb_logits.shape
    lb = pl.pallas_call(
        functools.partial(_lower_bound_kernel, depth=depth),
        out_shape=jax.ShapeDtypeStruct((dirs * depth, hw), F32),
        name="lower_bounds",
    )(lb_logits.astype(F32).reshape(dirs * depth, hw))
    return lb.reshape(dirs, depth, hw)


def _split2_np(m):
    hi = m.astype(ml_dtypes.bfloat16).astype(np.float64)
    lo = m - hi
    return hi, lo


def _split2(x):
    hi = x.astype(BF16)
    lo = (x - hi.astype(F32)).astype(BF16)
    return hi, lo


def _dft_constants(s, gd):
    nr = DFT_BLOCK
    kr = np.arange(nr, dtype=np.float64)
    cr = np.cos(2 * np.pi * np.outer(kr, kr) / nr)
    sr = np.sin(2 * np.pi * np.outer(kr, kr) / nr)
    mr = np.block([[cr, sr], [-sr, cr]])
    mrh, mrl = _split2_np(mr)
    mrcat = np.concatenate([mrh, mrh, mrl], axis=1)

    kc = np.arange(gd, dtype=np.float64)
    scale = 1.0 / np.sqrt(float(s) * gd)
    cc = np.cos(2 * np.pi * np.outer(kc, kc) / gd) * scale
    sc = np.sin(2 * np.pi * np.outer(kc, kc) / gd) * scale
    mc = np.concatenate([cc, sc], axis=0)
    mch, mcl = _split2_np(mc)
    mccat = np.concatenate([mch, mch, mcl], axis=0)

    k2 = np.arange(s // nr, dtype=np.float64)
    ang = (2 * np.pi * np.outer(k2, kr) / s).reshape(s, 1)
    twc = jnp.broadcast_to(jnp.asarray(np.cos(ang), F32), (s, LANES))
    tws = jnp.broadcast_to(jnp.asarray(np.sin(ang), F32), (s, LANES))
    return jnp.asarray(mrcat, BF16), jnp.asarray(mccat, BF16), twc, tws


def _c_add(a, b):
    return b if a is None else a if b is None else a + b


def _c_sub(a, b):
    return a if b is None else -b if a is None else a - b


def _fft_blocks(xs):
    n = len(xs)
    if n == 1:
        return xs
    even, odd = _fft_blocks(xs[0::2]), _fft_blocks(xs[1::2])
    out = [None] * n
    for k in range(n // 2):
        o_r, o_i = odd[k]
        if k == 0:
            t_r, t_i = o_r, o_i
        elif 4 * k == n:
            t_r, t_i = o_i, (None if o_r is None else -o_r)
        else:
            wr, wi = float(np.cos(2 * np.pi * k / n)), float(np.sin(2 * np.pi * k / n))
            t_r = _c_add(None if o_r is None else wr * o_r, None if o_i is None else wi * o_i)
            t_i = _c_sub(None if o_i is None else wr * o_i, None if o_r is None else wi * o_r)
        e_r, e_i = even[k]
        out[k] = (_c_add(e_r, t_r), _c_add(e_i, t_i))
        out[k + n // 2] = (_c_sub(e_r, t_r), _c_sub(e_i, t_i))
    return out


def _dft_kernel(x_ref, twc_ref, tws_ref, mr_ref, mc_ref, o_ref):
    nr = DFT_BLOCK
    nb = x_ref.shape[0] // nr

    def rows(ref, j):
        return ref[pl.ds(j * nr, nr), :]

    gd = x_ref.shape[1]
    u = _fft_blocks([(rows(x_ref, j), None) for j in range(nb)])

    def twiddled(k2):
        u_r, u_i = u[k2]
        u_r = jnp.zeros((nr, gd), F32) if u_r is None else u_r
        u_i = jnp.zeros_like(u_r) if u_i is None else u_i
        if k2 == 0:
            return jnp.concatenate([u_r, u_i], axis=0)
        c, s = rows(twc_ref, k2), rows(tws_ref, k2)
        return jnp.concatenate([u_r * c + u_i * s, u_i * c - u_r * s], axis=0)

    group = min(nb, 2 * LANES // gd)
    for k0 in range(0, nb, group):
        th, tl = _split2(jnp.concatenate([twiddled(k0 + g) for g in range(group)], axis=1))
        v = jnp.dot(mr_ref[...], jnp.concatenate([th, tl, th], axis=0),
                    preferred_element_type=F32)
        for g in range(group):
            sl = slice(g * gd, (g + 1) * gd)
            vh, vl = _split2(jnp.concatenate([v[:nr, sl], v[nr:, sl]], axis=1))
            o_ref[0, :, k0 + g, :] = jnp.dot(jnp.concatenate([vh, vl, vh], axis=1), mc_ref[...],
                                             preferred_element_type=F32)


def _fourier_call(act, consts, *, row0, batch, seq, col0, width):
    gd = FOURIER_GROUP_DIM
    nb = seq // DFT_BLOCK
    mrcat, mccat, twc, tws = consts

    def resident(a):
        return pl.BlockSpec(a.shape, lambda b, g: (0, 0), pipeline_mode=pl.Buffered(1))

    y = pl.pallas_call(
        _dft_kernel,
        grid=(batch, width // gd),
        in_specs=[pl.BlockSpec((seq, gd), lambda b, g: (row0 // seq + b, col0 // gd + g)),
                  resident(twc), resident(tws), resident(mrcat), resident(mccat)],
        out_specs=pl.BlockSpec((1, DFT_BLOCK, nb, gd), lambda b, g: (b, 0, 0, g)),
        out_shape=jax.ShapeDtypeStruct((batch, DFT_BLOCK, nb, width), F32),
        compiler_params=_cparams(("parallel", "parallel")),
        name="dft2d",
    )(act, twc, tws, mrcat, mccat)
    return y.reshape(batch * seq, width)


def _silu(x):
    h = 0.5 * x
    return h + h * jnp.tanh(h)


def _gate_consts(lb):
    c1 = 0.5 * (1.0 - lb)
    return lb + c1, c1


def _gates(z, c0, c1):
    f = c0 + c1 * jnp.tanh(0.5 * z)
    f_floor = jnp.maximum(f, MIN_FORGET)
    return f_floor, jnp.log2(f_floor), 1.0 - f


def _cumsum_rows(x, tri):
    w = x.shape[1]
    hi = x.astype(BF16)
    lo = (x - hi.astype(F32)).astype(BF16)
    y = jnp.dot(tri, jnp.concatenate([hi, lo], axis=1), preferred_element_type=F32)
    return y[:, :w] + y[:, w:]


def _boundary_rows(x, n, row8):
    c, w = x.shape
    parts = []
    if 2 * n >= SUBLANES:
        for p in range(c // (2 * n)):
            r = p * 2 * n + n - 1
            parts.append(jnp.broadcast_to(x[r:r + 1, :], (2 * n, w)))
    else:
        lo = row8 < 4
        for p in range(c // SUBLANES):
            r = p * SUBLANES
            a = jnp.broadcast_to(x[r + 1:r + 2, :], (SUBLANES, w))
            b = jnp.broadcast_to(x[r + 5:r + 6, :], (SUBLANES, w))
            parts.append(jnp.where(lo, a, b))
    return jnp.concatenate(parts, axis=0)


def _nt_dot(a, b):
    return lax.dot_general(a, b, (((1,), (1,)), ((), ())), preferred_element_type=F32)


def _intra_scores(q, kf, kb, ff, fb, b_f, bx_b, b_b, row, row8, xor):
    c = q.shape[0]
    products = []
    n = c // 2
    while n >= 1:
        if n >= SUBLANES:
            qt_parts, kt_parts = [], []
            for lo in range(0, c, 2 * n):
                mid, hi = lo + n, lo + 2 * n
                bf_m, bb_m = b_f[mid - 1:mid, :], b_b[mid - 1:mid, :]
                qt_parts.append(q[lo:mid] * jnp.exp2(bb_m - bx_b[lo:mid]))
                kt_parts.append(kf[lo:mid] * jnp.exp2(bf_m - b_f[lo:mid]))
                qt_parts.append(q[mid:hi] * jnp.exp2(b_f[mid:hi] - bf_m))
                kt_parts.append(kb[mid:hi] * jnp.exp2(bx_b[mid:hi] - bb_m))
            qt, kt = jnp.concatenate(qt_parts, axis=0), jnp.concatenate(kt_parts, axis=0)
        else:
            odd = (row & n) != 0
            kk = jnp.where(odd, kb, kf)
            if n == 1:
                qt = q * jnp.where(odd, ff, fb)
                kt = kk
            else:
                d1 = b_f - _boundary_rows(b_f, n, row8)
                d2 = _boundary_rows(b_b, n, row8) - bx_b
                qt = q * jnp.exp2(jnp.minimum(d1, d2))
                kt = kk * jnp.exp2(-jnp.maximum(d1, d2))
        products.append((n, _nt_dot(qt.astype(BF16), kt.astype(BF16))))
        n //= 2
    return products


def _assemble_scores(products, xor):
    p = None
    for n, pn in products:
        p = pn if p is None else jnp.where(xor < 2 * n, pn, p)
    return jnp.where(xor == 0, 0.0, p)


def _scan_fwd_kernel(q_ref, zf_ref, zb_ref, v_ref, lbf_ref, lbb_ref,
                     o_ref, qb_ref, kb_ref, tot_ref, vt_ref, st_ref, *, chunk):
    @pl.when(pl.program_id(2) == 0)
    def _():
        st_ref[...] = jnp.zeros_like(st_ref)

    c = chunk
    w = q_ref.shape[1]
    ri = lax.broadcasted_iota(jnp.int32, (c, c), 0)
    ci = lax.broadcasted_iota(jnp.int32, (c, c), 1)
    tri = (ri >= ci).astype(BF16)
    xor = ri ^ ci
    row = lax.broadcasted_iota(jnp.int32, (c, w), 0)
    row8 = lax.broadcasted_iota(jnp.int32, (SUBLANES, w), 0)
    c0f, c1f = _gate_consts(lbf_ref[...])
    c0b, c1b = _gate_consts(lbb_ref[...])

    nch = q_ref.shape[0] // c
    work = [dict(sl=pl.ds(ch * c, c), ch=ch) for ch in range(nch)]

    def stage_gates(s):
        sl = s["sl"]
        s["q"] = _silu(q_ref[sl, :])
        s["v"] = v_ref[sl, :]
        s["ff"], lf, s["kf"] = _gates(zf_ref[sl, :], c0f, c1f)
        s["fb"], s["lfb"], s["kb"] = _gates(zb_ref[sl, :], c0b, c1b)
        s["b_f"] = _cumsum_rows(lf, tri)
        s["b_b"] = _cumsum_rows(s["lfb"], tri)

    def stage_levels(s):
        q, v, kf, kb, b_f, b_b = s["q"], s["v"], s["kf"], s["kb"], s["b_f"], s["b_b"]
        bx_b = b_b - s["lfb"]
        s["products"] = _intra_scores(q, kf, kb, s["ff"], s["fb"], b_f, bx_b, b_b, row, row8, xor)
        s["diag"] = jnp.sum(q * (kf + kb), axis=-1, keepdims=True)
        b_last = b_f[c - 1:c, :]
        s["qh"] = (q * jnp.exp2(b_f)).astype(BF16)
        kh = (kf * jnp.exp2(b_last - b_f)).astype(BF16)
        vt = v.T.astype(BF16)
        vt_ref[:, s["sl"]] = vt
        s["upd"] = jnp.dot(vt, kh, preferred_element_type=F32)
        s["decay"] = jnp.exp2(b_last)
        tot = b_b[c - 1:c, :]
        qb_ref[s["sl"], :] = (q * jnp.exp2(tot - bx_b)).astype(BF16)
        kb_ref[s["sl"], :] = (kb * jnp.exp2(bx_b)).astype(BF16)
        tot_ref[pl.ds(s["ch"] * SUBLANES, SUBLANES), :] = jnp.broadcast_to(jnp.exp2(tot),
                                                                            (SUBLANES, w))

    def stage_scores(s):
        p = _assemble_scores(s.pop("products"), xor)
        s["pv"] = jnp.dot(p.astype(BF16), s["v"].astype(BF16), preferred_element_type=F32)

    stages = (stage_gates, stage_levels, stage_scores)
    for step in range(nch + len(stages) - 1):
        for k, stage in enumerate(stages):
            if 0 <= step - k < nch:
                stage(work[step - k])

    st = st_ref[...]
    for s in work:
        o_ref[s["sl"], :] = s["pv"] + s["diag"] * s["v"] + _nt_dot(s["qh"], st.astype(BF16))
        st = st * s["decay"] + s["upd"]
    st_ref[...] = st


def _scan_bwd_kernel(qb_ref, kb_ref, tot_ref, vt_ref, g_ref, o1_ref, hg_ref, o_ref, st_ref, *,
                     chunk, lookahead):
    @pl.when(pl.program_id(2) == 0)
    def _():
        st_ref[...] = jnp.zeros_like(st_ref)

    c = chunk
    hd = HEAD_DIM
    nch = o1_ref.shape[0] // c
    items = [(hh, ch) for hh in range(st_ref.shape[0]) for ch in reversed(range(nch))]
    rows = {ch: pl.ds(ch * c, c) for ch in range(nch)}
    lanes = {hh: pl.ds(hh * hd, hd) for hh in range(st_ref.shape[0])}

    upd = {(hh, ch): jnp.dot(vt_ref[lanes[hh], rows[ch]], kb_ref[rows[ch], lanes[hh]],
                             preferred_element_type=F32) for hh, ch in items}
    st_in = {}
    for hh in range(st_ref.shape[0]):
        st = st_ref[hh]
        for ch in reversed(range(nch)):
            st_in[hh, ch] = st.astype(BF16)
            st = st * tot_ref[pl.ds(ch * SUBLANES, 1), lanes[hh]] + upd[hh, ch]
        st_ref[hh] = st

    inter = {}
    for i in range(len(items) + lookahead):
        if i < len(items):
            hh, ch = items[i]
            inter[hh, ch] = _nt_dot(qb_ref[rows[ch], lanes[hh]], st_in.pop((hh, ch)))
        if i >= lookahead:
            hh, ch = items[i - lookahead]
            o = o1_ref[rows[ch], lanes[hh]] + inter.pop((hh, ch))
            o = o * lax.rsqrt(jnp.mean(o * o, axis=-1, keepdims=True) + NORM_EPS)
            o_ref[rows[ch], lanes[hh]] = (o * hg_ref[:, lanes[hh]]
                                          * _silu(g_ref[rows[ch], lanes[hh]])).astype(BF16)


def _scan_call(mid, lb_f, lb_b, hg_gain, *, row0, batch, seq, heads, cols, tc, tc_bwd,
               heads_per_step_bwd):
    hd = HEAD_DIM
    nb = seq // tc
    rb0 = row0 // tc
    cq, czf, czb, cv, cg = (c // hd for c in cols)
    assert heads % heads_per_step_bwd == 0 and cg % heads_per_step_bwd == 0
    rows_per_tot = SCAN_CHUNK // SUBLANES
    hw = heads * hd

    def tok(col):
        return pl.BlockSpec((tc, hd), lambda b, h, j: (rb0 + b * nb + j, col + h))

    def own(rows):
        return pl.BlockSpec((rows, hd), lambda b, h, j: (b * nb + j, h))

    par = pl.BlockSpec((1, hd), lambda b, h, j: (0, h))
    o1, qb, kb, tot, vt = pl.pallas_call(
        functools.partial(_scan_fwd_kernel, chunk=SCAN_CHUNK),
        grid=(batch, heads, nb),
        in_specs=[tok(cq), tok(czf), tok(czb), tok(cv), par, par],
        out_specs=[own(tc), own(tc), own(tc), own(tc // rows_per_tot),
                   pl.BlockSpec((hd, tc), lambda b, h, j: (h, b * nb + j))],
        out_shape=[jax.ShapeDtypeStruct((batch * seq, hw), F32),
                   jax.ShapeDtypeStruct((batch * seq, hw), BF16),
                   jax.ShapeDtypeStruct((batch * seq, hw), BF16),
                   jax.ShapeDtypeStruct((batch * seq // rows_per_tot, hw), F32),
                   jax.ShapeDtypeStruct((hw, batch * seq), BF16)],
        scratch_shapes=[pltpu.VMEM((hd, hd), F32)],
        compiler_params=_cparams(("parallel", "parallel", "arbitrary")),
        name="hgrn2_fwd",
    )(mid, mid, mid, mid, lb_f, lb_b)

    nb2 = seq // tc_bwd
    rb2 = row0 // tc_bwd

    hps = heads_per_step_bwd
    wd = hd * hps

    def own_rev(rows):
        return pl.BlockSpec((rows, wd), lambda b, h, j: (b * nb2 + nb2 - 1 - j, h))

    return pl.pallas_call(
        functools.partial(_scan_bwd_kernel, chunk=SCAN_CHUNK, lookahead=2),
        grid=(batch, heads // hps, nb2),
        in_specs=[own_rev(tc_bwd), own_rev(tc_bwd), own_rev(tc_bwd // rows_per_tot),
                  pl.BlockSpec((wd, tc_bwd), lambda b, h, j: (h, b * nb2 + nb2 - 1 - j)),
                  pl.BlockSpec((tc_bwd, wd),
                               lambda b, h, j: (rb2 + b * nb2 + nb2 - 1 - j, cg // hps + h)),
                  own_rev(tc_bwd), pl.BlockSpec((1, wd), lambda b, h, j: (0, h))],
        out_specs=own_rev(tc_bwd),
        out_shape=jax.ShapeDtypeStruct((batch * seq, hw), BF16),
        scratch_shapes=[pltpu.VMEM((hps, hd, hd), F32)],
        compiler_params=_cparams(("parallel", "parallel", "arbitrary")),
        name="hgrn2_bwd",
    )(qb, kb, tot, vt, mid, o1, hg_gain)


def _trunk(x, groups, params, *, ffn_tm, ffn_tf, proj_tm, proj_tn, out_tm, scan_tc, scan_tc_bwd):
    (ffn1_norm, ffn1_w_in, ffn1_w_down, mix_norm, w_in, lb_logits, hg_norm, w_fourier_out,
     w_hgrn_out, w_out, ffn2_norm, ffn2_w_in, ffn2_w_down, final_norm) = params
    depth, d = mix_norm.shape
    fw = w_fourier_out.shape[1]
    hw = w_hgrn_out.shape[1]
    heads = hw // HEAD_DIM
    d_ff = ffn1_w_down.shape[1]
    cols = tuple(j * hw for j in (0, 1, 2, 3, 4))

    def ffn_half(x, norm, w_in_l, w_down_l):
        return _ffn_call(x, norm[None, :], w_in_l[:, :d_ff].astype(BF16),
                         w_in_l[:, d_ff:].astype(BF16), w_down_l.astype(BF16),
                         tm=ffn_tm, tf=ffn_tf)

    lb = _lower_bound_call(lb_logits)
    dft = {seq: _dft_constants(seq, FOURIER_GROUP_DIM) for (_, _, seq) in groups}

    for l in range(depth):
        x = ffn_half(x, ffn1_norm[l], ffn1_w_in[l], ffn1_w_down[l])

        mid, gates = _proj_call(x, mix_norm[l][None, :], w_in[l].astype(BF16), tm=proj_tm,
                                tn=proj_tn, n_fourier=fw, n_gates=2 * d)
        fm_parts, ob_parts = [], []
        for (row0, batch, seq) in groups:
            fm_parts.append(_fourier_call(mid, dft[seq], row0=row0, batch=batch, seq=seq,
                                          col0=5 * hw, width=fw))
            ob_parts.append(_scan_call(mid, lb[0, l][None, :], lb[1, l][None, :],
                                       hg_norm[l][None, :], row0=row0, batch=batch, seq=seq,
                                       heads=heads, cols=cols, tc=scan_tc,
                                       tc_bwd=min(scan_tc_bwd, seq), heads_per_step_bwd=4))
        x = _mixout_call(x, fm_parts, ob_parts, gates, w_fourier_out[l].astype(BF16),
                         w_hgrn_out[l].astype(BF16), w_out[l].astype(BF16), tm=out_tm)

        x = ffn_half(x, ffn2_norm[l], ffn2_w_in[l], ffn2_w_down[l])

    return _norm_call(x, final_norm[None, :], [batch * seq for (_, batch, seq) in groups],
                      tm=out_tm)


def kernel(x_prompt, x_sample, ffn1_norm, ffn1_w_in, ffn1_w_down, mix_norm, w_in, lb_logits, hg_norm, w_fourier_out, w_hgrn_out, w_out, ffn2_norm, ffn2_w_in, ffn2_w_down, final_norm):
    bp, sp, d = x_prompt.shape
    bs, ss, _ = x_sample.shape
    x = jnp.concatenate([x_prompt.reshape(bp * sp, d), x_sample.reshape(bs * ss, d)], axis=0)
    groups = [(0, bp, sp), (bp * sp, bs, ss)]
    params = (ffn1_norm, ffn1_w_in, ffn1_w_down, mix_norm, w_in, lb_logits, hg_norm,
              w_fourier_out, w_hgrn_out, w_out, ffn2_norm, ffn2_w_in, ffn2_w_down, final_norm)
    y_prompt, y_sample = _trunk(x, groups, params, ffn_tm=1024, ffn_tf=512, proj_tm=1024,
                                proj_tn=1024, out_tm=256, scan_tc=1024, scan_tc_bwd=2048)
    return (y_prompt.reshape(bp, sp, d), y_sample.reshape(bs, ss, d))
```

```python
import functools

import ml_dtypes
import numpy as np
import jax
import jax.numpy as jnp
from jax import lax
from jax.experimental import pallas as pl
from jax.experimental.pallas import tpu as pltpu

F32 = jnp.float32
BF16 = jnp.bfloat16

NORM_EPS = 1e-6
MIN_FORGET = 1e-30
LANES = 128
SUBLANES = 8
HEAD_DIM = 128
FOURIER_GROUP_DIM = 128
SCAN_CHUNK = 128
DFT_BLOCK = 128
VMEM_LIMIT = 56 * 1024 * 1024


def _cparams(sem):
    return pltpu.CompilerParams(dimension_semantics=sem, vmem_limit_bytes=VMEM_LIMIT)


def _rms_scale(x, gain):
    ms = jnp.mean(x * x, axis=-1, keepdims=True)
    return x * lax.rsqrt(ms + NORM_EPS) * gain


def _ffn_kernel(x_ref, gain_ref, wg_ref, wu_ref, wd_ref, o_ref, h_ref, *, last_cols):
    j = pl.program_id(1)
    last = pl.num_programs(1) - 1
    tf = wg_ref.shape[1]

    @pl.when(j == 0)
    def _():
        x = x_ref[...]
        h_ref[...] = _rms_scale(x, gain_ref[...]).astype(BF16)
        o_ref[...] = x

    def step(cols):
        h = h_ref[...]
        g = jnp.dot(h, wg_ref[:, :cols], preferred_element_type=F32)
        u = jnp.dot(h, wu_ref[:, :cols], preferred_element_type=F32)
        a = (g * jax.nn.sigmoid(g) * u * 0.5).astype(BF16)
        o_ref[...] += jnp.dot(a, wd_ref[:cols, :], preferred_element_type=F32)

    pl.when(j < last)(lambda: step(tf))
    pl.when(j == last)(lambda: step(last_cols))


def _ffn_call(x, gain, w_gate, w_up, w_down, *, tm, tf):
    t, d = x.shape
    f = w_down.shape[0]
    nf = pl.cdiv(f, tf)
    last_cols = f - (nf - 1) * tf
    assert last_cols % LANES == 0
    return pl.pallas_call(
        functools.partial(_ffn_kernel, last_cols=last_cols),
        grid=(t // tm, nf),
        in_specs=[
            pl.BlockSpec((tm, d), lambda i, j: (i, 0)),
            pl.BlockSpec((1, d), lambda i, j: (0, 0)),
            pl.BlockSpec((d, tf), lambda i, j: (0, j)),
            pl.BlockSpec((d, tf), lambda i, j: (0, j)),
            pl.BlockSpec((tf, d), lambda i, j: (j, 0)),
        ],
        out_specs=pl.BlockSpec((tm, d), lambda i, j: (i, 0)),
        out_shape=jax.ShapeDtypeStruct((t, d), F32),
        scratch_shapes=[pltpu.VMEM((tm, d), BF16)],
        compiler_params=_cparams(("parallel", "arbitrary")),
        name="ffn_half",
    )(x, gain, w_gate, w_up, w_down)


def _proj_kernel(x_ref, gain_ref, wlo_ref, whi_ref, act_ref, gate_ref, h_ref, *, j_gate):
    j = pl.program_id(1)
    half = wlo_ref.shape[1]

    @pl.when(j == 0)
    def _():
        h_ref[...] = _rms_scale(x_ref[...], gain_ref[...]).astype(BF16)

    def tile(o_ref):
        h = h_ref[...]
        o_ref[:, :half] = jnp.dot(h, wlo_ref[...], preferred_element_type=F32)
        o_ref[:, half:] = jnp.dot(h, whi_ref[...], preferred_element_type=F32)

    pl.when(j < j_gate)(lambda: tile(act_ref))
    pl.when(j >= j_gate)(lambda: tile(gate_ref))


def _proj_call(x, gain, w, *, tm, tn, n_fourier, n_gates):
    t, d = x.shape
    n = w.shape[1]
    half = tn // 2
    n_act = n - n_gates
    j_gate, n_steps = n_act // tn, n // tn
    nfb, nmb = n_fourier // half, (n_act - n_fourier) // half
    assert n_act % tn == 0 and n_gates % tn == 0 and n_fourier % half == 0

    def source_block(a):
        return jnp.where(a < nmb, a + nfb, jnp.where(a < nmb + nfb, a - nmb, a))

    return pl.pallas_call(
        functools.partial(_proj_kernel, j_gate=j_gate),
        grid=(t // tm, n_steps),
        in_specs=[
            pl.BlockSpec((tm, d), lambda i, j: (i, 0)),
            pl.BlockSpec((1, d), lambda i, j: (0, 0)),
            pl.BlockSpec((d, half), lambda i, j: (0, source_block(2 * j))),
            pl.BlockSpec((d, half), lambda i, j: (0, source_block(2 * j + 1))),
        ],
        out_specs=[
            pl.BlockSpec((tm, tn), lambda i, j: (i, jnp.minimum(j, j_gate - 1))),
            pl.BlockSpec((tm, tn), lambda i, j: (i, jnp.maximum(j - j_gate, 0))),
        ],
        out_shape=[jax.ShapeDtypeStruct((t, n_act), F32),
                   jax.ShapeDtypeStruct((t, n_gates), F32)],
        scratch_shapes=[pltpu.VMEM((tm, d), BF16)],
        compiler_params=_cparams(("parallel", "arbitrary")),
        name="mixer_proj",
    )(x, gain, w, w)


def _sigmoid(x):
    return 0.5 + 0.5 * jnp.tanh(0.5 * x)


def _mixout_kernel(*refs, tile_starts):
    ng = len(tile_starts)
    x_ref = refs[0]
    fm_refs = refs[1:1 + ng]
    ob_refs = refs[1 + ng:1 + 2 * ng]
    ga_ref, gb_ref, wfo_ref, who_ref, wo_ref, o_ref = refs[1 + 2 * ng:]
    i = pl.program_id(0)
    fm, ob = fm_refs[0][...], ob_refs[0][...]
    for g in range(1, ng):
        in_g = i >= tile_starts[g]
        fm = jnp.where(in_g, fm_refs[g][...], fm)
        ob = jnp.where(in_g, ob_refs[g][...], ob)
    a = jnp.dot(fm.astype(BF16), wfo_ref[...], preferred_element_type=F32)
    b = jnp.dot(ob, who_ref[...], preferred_element_type=F32)
    m = (_sigmoid(ga_ref[...]) * a + _sigmoid(gb_ref[...]) * b).astype(BF16)
    o_ref[...] = x_ref[...] + jnp.dot(m, wo_ref[...], preferred_element_type=F32)


def _mixout_call(x, fm_parts, ob_parts, gates, w_fo, w_ho, w_out, *, tm):
    t, d = x.shape
    tile_starts, tile_counts, start = [], [], 0
    for part in fm_parts:
        tile_starts.append(start)
        tile_counts.append(part.shape[0] // tm)
        start += part.shape[0] // tm

    def group_spec(part, g):
        return pl.BlockSpec(
            (tm, part.shape[1]),
            lambda i: (jnp.clip(i - tile_starts[g], 0, tile_counts[g] - 1), 0))

    def resident(w):
        return pl.BlockSpec(w.shape, lambda i: (0, 0), pipeline_mode=pl.Buffered(1))

    return pl.pallas_call(
        functools.partial(_mixout_kernel, tile_starts=tuple(tile_starts)),
        grid=(t // tm,),
        in_specs=(
            [pl.BlockSpec((tm, d), lambda i: (i, 0))]
            + [group_spec(p, g) for g, p in enumerate(fm_parts)]
            + [group_spec(p, g) for g, p in enumerate(ob_parts)]
            + [pl.BlockSpec((tm, d), lambda i: (i, 0)), pl.BlockSpec((tm, d), lambda i: (i, 1)),
               resident(w_fo), resident(w_ho), resident(w_out)]),
        out_specs=pl.BlockSpec((tm, d), lambda i: (i, 0)),
        out_shape=jax.ShapeDtypeStruct((t, d), F32),
        compiler_params=_cparams(("parallel",)),
        name="mixer_out",
    )(x, *fm_parts, *ob_parts, gates, gates, w_fo, w_ho, w_out)


def _norm_kernel(x_ref, gain_ref, *o_refs, tile_starts):
    i = pl.program_id(0)
    bounds = tuple(tile_starts) + (pl.num_programs(0),)
    for g, o_ref in enumerate(o_refs):
        @pl.when((i >= bounds[g]) & (i < bounds[g + 1]))
        def _():
            o_ref[...] = _rms_scale(x_ref[...], gain_ref[...])


def _norm_call(x, gain, group_rows, *, tm):
    t, d = x.shape
    tile_starts, tile_counts, start = [], [], 0
    for rows in group_rows:
        tile_starts.append(start)
        tile_counts.append(rows // tm)
        start += rows // tm

    def group_spec(g):
        return pl.BlockSpec(
            (tm, d), lambda i: (jnp.clip(i - tile_starts[g], 0, tile_counts[g] - 1), 0))

    return pl.pallas_call(
        functools.partial(_norm_kernel, tile_starts=tuple(tile_starts)),
        grid=(t // tm,),
        in_specs=[pl.BlockSpec((tm, d), lambda i: (i, 0)), pl.BlockSpec((1, d), lambda i: (0, 0))],
        out_specs=[group_spec(g) for g in range(len(group_rows))],
        out_shape=[jax.ShapeDtypeStruct((rows, d), F32) for rows in group_rows],
        compiler_params=_cparams(("arbitrary",)),
        name="final_norm",
    )(x, gain)


def _lower_bound_kernel(logit_ref, o_ref, *, depth):
    for d0 in range(0, logit_ref.shape[0], depth):
        rows = [logit_ref[d0 + l:d0 + l + 1, :] for l in range(depth)]
        mx = functools.reduce(jnp.maximum, rows)
        es = [jnp.exp(r - mx) for r in rows]
        tot = functools.reduce(lambda a, b: a + b, es)
        ps = [e / tot for e in es]
        acc = ps[0]
        for l in range(depth):
            if l:
                acc = acc + ps[l]
            o_ref[d0 + l:d0 + l + 1, :] = jnp.maximum(acc - ps[0], 0.0)


def _lower_bound_call(lb_logits):
    dirs, depth, hw = lb_logits.shape
    lb = pl.pallas_call(
        functools.partial(_lower_bound_kernel, depth=depth),
        out_shape=jax.ShapeDtypeStruct((dirs * depth, hw), F32),
        name="lower_bounds",
    )(lb_logits.astype(F32).reshape(dirs * depth, hw))
    return lb.reshape(dirs, depth, hw)


def _split2_np(m):
    hi = m.astype(ml_dtypes.bfloat16).astype(np.float64)
    lo = m - hi
    return hi, lo


def _split2(x):
    hi = x.astype(BF16)
    lo = (x - hi.astype(F32)).astype(BF16)
    return hi, lo


def _dft_constants(s, gd):
    nr = DFT_BLOCK
    kr = np.arange(nr, dtype=np.float64)
    cr = np.cos(2 * np.pi * np.outer(kr, kr) / nr)
    sr = np.sin(2 * np.pi * np.outer(kr, kr) / nr)
    mr = np.block([[cr, sr], [-sr, cr]])
    mrh, mrl = _split2_np(mr)
    mrcat = np.concatenate([mrh, mrh, mrl], axis=1)

    kc = np.arange(gd, dtype=np.float64)
    scale = 1.0 / np.sqrt(float(s) * gd)
    cc = np.cos(2 * np.pi * np.outer(kc, kc) / gd) * scale
    sc = np.sin(2 * np.pi * np.outer(kc, kc) / gd) * scale
    mc = np.concatenate([cc, sc], axis=0)
    mch, mcl = _split2_np(mc)
    mccat = np.concatenate([mch, mch, mcl], axis=0)

    k2 = np.arange(s // nr, dtype=np.float64)
    ang = (2 * np.pi * np.outer(k2, kr) / s).reshape(s, 1)
    twc = jnp.broadcast_to(jnp.asarray(np.cos(ang), F32), (s, LANES))
    tws = jnp.broadcast_to(jnp.asarray(np.sin(ang), F32), (s, LANES))
    return jnp.asarray(mrcat, BF16), jnp.asarray(mccat, BF16), twc, tws


def _c_add(a, b):
    return b if a is None else a if b is None else a + b


def _c_sub(a, b):
    return a if b is None else -b if a is None else a - b


def _fft_blocks(xs):
    n = len(xs)
    if n == 1:
        return xs
    even, odd = _fft_blocks(xs[0::2]), _fft_blocks(xs[1::2])
    out = [None] * n
    for k in range(n // 2):
        o_r, o_i = odd[k]
        if k == 0:
            t_r, t_i = o_r, o_i
        elif 4 * k == n:
            t_r, t_i = o_i, (None if o_r is None else -o_r)
        else:
            wr, wi = float(np.cos(2 * np.pi * k / n)), float(np.sin(2 * np.pi * k / n))
            t_r = _c_add(None if o_r is None else wr * o_r, None if o_i is None else wi * o_i)
            t_i = _c_sub(None if o_i is None else wr * o_i, None if o_r is None else wi * o_r)
        e_r, e_i = even[k]
        out[k] = (_c_add(e_r, t_r), _c_add(e_i, t_i))
        out[k + n // 2] = (_c_sub(e_r, t_r), _c_sub(e_i, t_i))
    return out


def _dft_kernel(x_ref, twc_ref, tws_ref, mr_ref, mc_ref, o_ref):
    nr = DFT_BLOCK
    nb = x_ref.shape[0] // nr

    def rows(ref, j):
        return ref[pl.ds(j * nr, nr), :]

    gd = x_ref.shape[1]
    u = _fft_blocks([(rows(x_ref, j), None) for j in range(nb)])

    def twiddled(k2):
        u_r, u_i = u[k2]
        u_r = jnp.zeros((nr, gd), F32) if u_r is None else u_r
        u_i = jnp.zeros_like(u_r) if u_i is None else u_i
        if k2 == 0:
            return jnp.concatenate([u_r, u_i], axis=0)
        c, s = rows(twc_ref, k2), rows(tws_ref, k2)
        return jnp.concatenate([u_r * c + u_i * s, u_i * c - u_r * s], axis=0)

    group = min(nb, 2 * LANES // gd)
    for k0 in range(0, nb, group):
        th, tl = _split2(jnp.concatenate([twiddled(k0 + g) for g in range(group)], axis=1))
        v = jnp.dot(mr_ref[...], jnp.concatenate([th, tl, th], axis=0),
                    preferred_element_type=F32)
        for g in range(group):
            sl = slice(g * gd, (g + 1) * gd)
            vh, vl = _split2(jnp.concatenate([v[:nr, sl], v[nr:, sl]], axis=1))
            o_ref[0, :, k0 + g, :] = jnp.dot(jnp.concatenate([vh, vl, vh], axis=1), mc_ref[...],
                                             preferred_element_type=F32)


def _fourier_call(act, consts, *, row0, batch, seq, col0, width):
    gd = FOURIER_GROUP_DIM
    nb = seq // DFT_BLOCK
    mrcat, mccat, twc, tws = consts

    def resident(a):
        return pl.BlockSpec(a.shape, lambda b, g: (0, 0), pipeline_mode=pl.Buffered(1))

    y = pl.pallas_call(
        _dft_kernel,
        grid=(batch, width // gd),
        in_specs=[pl.BlockSpec((seq, gd), lambda b, g: (row0 // seq + b, col0 // gd + g)),
                  resident(twc), resident(tws), resident(mrcat), resident(mccat)],
        out_specs=pl.BlockSpec((1, DFT_BLOCK, nb, gd), lambda b, g: (b, 0, 0, g)),
        out_shape=jax.ShapeDtypeStruct((batch, DFT_BLOCK, nb, width), F32),
        compiler_params=_cparams(("parallel", "parallel")),
        name="dft2d",
    )(act, twc, tws, mrcat, mccat)
    return y.reshape(batch * seq, width)


def _silu(x):
    h = 0.5 * x
    return h + h * jnp.tanh(h)


def _gate_consts(lb):
    c1 = 0.5 * (1.0 - lb)
    return lb + c1, c1


def _gates(z, c0, c1):
    f = c0 + c1 * jnp.tanh(0.5 * z)
    f_floor = jnp.maximum(f, MIN_FORGET)
    return f_floor, jnp.log2(f_floor), 1.0 - f


def _cumsum_rows(x, tri):
    w = x.shape[1]
    hi = x.astype(BF16)
    lo = (x - hi.astype(F32)).astype(BF16)
    y = jnp.dot(tri, jnp.concatenate([hi, lo], axis=1), preferred_element_type=F32)
    return y[:, :w] + y[:, w:]


def _boundary_rows(x, n, row8):
    c, w = x.shape
    parts = []
    if 2 * n >= SUBLANES:
        for p in range(c // (2 * n)):
            r = p * 2 * n + n - 1
            parts.append(jnp.broadcast_to(x[r:r + 1, :], (2 * n, w)))
    else:
        lo = row8 < 4
        for p in range(c // SUBLANES):
            r = p * SUBLANES
            a = jnp.broadcast_to(x[r + 1:r + 2, :], (SUBLANES, w))
            b = jnp.broadcast_to(x[r + 5:r + 6, :], (SUBLANES, w))
            parts.append(jnp.where(lo, a, b))
    return jnp.concatenate(parts, axis=0)


def _nt_dot(a, b):
    return lax.dot_general(a, b, (((1,), (1,)), ((), ())), preferred_element_type=F32)


def _intra_scores(q, kf, kb, ff, fb, b_f, bx_b, b_b, row, row8, xor):
    c = q.shape[0]
    products = []
    n = c // 2
    while n >= 1:
        if n >= SUBLANES:
            qt_parts, kt_parts = [], []
            for lo in range(0, c, 2 * n):
                mid, hi = lo + n, lo + 2 * n
                bf_m, bb_m = b_f[mid - 1:mid, :], b_b[mid - 1:mid, :]
                qt_parts.append(q[lo:mid] * jnp.exp2(bb_m - bx_b[lo:mid]))
                kt_parts.append(kf[lo:mid] * jnp.exp2(bf_m - b_f[lo:mid]))
                qt_parts.append(q[mid:hi] * jnp.exp2(b_f[mid:hi] - bf_m))
                kt_parts.append(kb[mid:hi] * jnp.exp2(bx_b[mid:hi] - bb_m))
            qt, kt = jnp.concatenate(qt_parts, axis=0), jnp.concatenate(kt_parts, axis=0)
        else:
            odd = (row & n) != 0
            kk = jnp.where(odd, kb, kf)
            if n == 1:
                qt = q * jnp.where(odd, ff, fb)
                kt = kk
            else:
                d1 = b_f - _boundary_rows(b_f, n, row8)
                d2 = _boundary_rows(b_b, n, row8) - bx_b
                qt = q * jnp.exp2(jnp.minimum(d1, d2))
                kt = kk * jnp.exp2(-jnp.maximum(d1, d2))
        products.append((n, _nt_dot(qt.astype(BF16), kt.astype(BF16))))
        n //= 2
    return products


def _assemble_scores(products, xor):
    p = None
    for n, pn in products:
        p = pn if p is None else jnp.where(xor < 2 * n, pn, p)
    return jnp.where(xor == 0, 0.0, p)


def _scan_fwd_kernel(q_ref, zf_ref, zb_ref, v_ref, lbf_ref, lbb_ref,
                     o_ref, qb_ref, kb_ref, tot_ref, vt_ref, st_ref, *, chunk):
    @pl.when(pl.program_id(2) == 0)
    def _():
        st_ref[...] = jnp.zeros_like(st_ref)

    c = chunk
    w = HEAD_DIM
    n_heads = st_ref.shape[0]
    ri = lax.broadcasted_iota(jnp.int32, (c, c), 0)
    ci = lax.broadcasted_iota(jnp.int32, (c, c), 1)
    tri = (ri >= ci).astype(BF16)
    xor = ri ^ ci
    row = lax.broadcasted_iota(jnp.int32, (c, w), 0)
    row8 = lax.broadcasted_iota(jnp.int32, (SUBLANES, w), 0)
    lanes = [pl.ds(hh * w, w) for hh in range(n_heads)]
    consts = [_gate_consts(lbf_ref[:, ln]) + _gate_consts(lbb_ref[:, ln]) for ln in lanes]

    nch = q_ref.shape[0] // c
    work = [dict(sl=pl.ds(ch * c, c), ch=ch, hh=hh) for hh in range(n_heads) for ch in range(nch)]

    def stage_gates(s):
        sl, ln = s["sl"], lanes[s["hh"]]
        c0f, c1f, c0b, c1b = consts[s["hh"]]
        s["q"] = _silu(q_ref[sl, ln])
        s["v"] = v_ref[sl, ln]
        s["ff"], lf, s["kf"] = _gates(zf_ref[sl, ln], c0f, c1f)
        s["fb"], s["lfb"], s["kb"] = _gates(zb_ref[sl, ln], c0b, c1b)
        s["b_f"] = _cumsum_rows(lf, tri)
        s["b_b"] = _cumsum_rows(s["lfb"], tri)

    def stage_levels(s):
        sl, ln = s["sl"], lanes[s["hh"]]
        q, v, kf, kb, b_f, b_b = s["q"], s["v"], s["kf"], s["kb"], s["b_f"], s["b_b"]
        bx_b = b_b - s["lfb"]
        s["products"] = _intra_scores(q, kf, kb, s["ff"], s["fb"], b_f, bx_b, b_b, row, row8, xor)
        s["diag"] = jnp.sum(q * (kf + kb), axis=-1, keepdims=True)
        b_last = b_f[c - 1:c, :]
        s["qh"] = (q * jnp.exp2(b_f)).astype(BF16)
        kh = (kf * jnp.exp2(b_last - b_f)).astype(BF16)
        vt = v.T.astype(BF16)
        vt_ref[ln, sl] = vt
        s["upd"] = jnp.dot(vt, kh, preferred_element_type=F32)
        s["decay"] = jnp.exp2(b_last)
        tot = b_b[c - 1:c, :]
        qb_ref[sl, ln] = (q * jnp.exp2(tot - bx_b)).astype(BF16)
        kb_ref[sl, ln] = (kb * jnp.exp2(bx_b)).astype(BF16)
        tot_ref[pl.ds(s["ch"] * SUBLANES, SUBLANES), ln] = jnp.broadcast_to(jnp.exp2(tot),
                                                                             (SUBLANES, w))

    def stage_scores(s):
        p = _assemble_scores(s.pop("products"), xor)
        s["pv"] = jnp.dot(p.astype(BF16), s["v"].astype(BF16), preferred_element_type=F32)

    stages = (stage_gates, stage_levels, stage_scores)
    for step in range(len(work) + len(stages) - 1):
        for k, stage in enumerate(stages):
            if 0 <= step - k < len(work):
                stage(work[step - k])

    for hh in range(n_heads):
        st = st_ref[hh]
        for s in work[hh * nch:(hh + 1) * nch]:
            o_ref[s["sl"], lanes[hh]] = (s["pv"] + s["diag"] * s["v"]
                                         + _nt_dot(s["qh"], st.astype(BF16)))
            st = st * s["decay"] + s["upd"]
        st_ref[hh] = st


def _scan_bwd_kernel(qb_ref, kb_ref, tot_ref, vt_ref, g_ref, o1_ref, hg_ref, o_ref, st_ref, *,
                     chunk, lookahead):
    @pl.when(pl.program_id(2) == 0)
    def _():
        st_ref[...] = jnp.zeros_like(st_ref)

    c = chunk
    hd = HEAD_DIM
    nch = o1_ref.shape[0] // c
    items = [(hh, ch) for hh in range(st_ref.shape[0]) for ch in reversed(range(nch))]
    rows = {ch: pl.ds(ch * c, c) for ch in range(nch)}
    lanes = {hh: pl.ds(hh * hd, hd) for hh in range(st_ref.shape[0])}

    upd = {(hh, ch): jnp.dot(vt_ref[lanes[hh], rows[ch]], kb_ref[rows[ch], lanes[hh]],
                             preferred_element_type=F32) for hh, ch in items}
    st_in = {}
    for hh in range(st_ref.shape[0]):
        st = st_ref[hh]
        for ch in reversed(range(nch)):
            st_in[hh, ch] = st.astype(BF16)
            st = st * tot_ref[pl.ds(ch * SUBLANES, 1), lanes[hh]] + upd[hh, ch]
        st_ref[hh] = st

    inter = {}
    for i in range(len(items) + lookahead):
        if i < len(items):
            hh, ch = items[i]
            inter[hh, ch] = _nt_dot(qb_ref[rows[ch], lanes[hh]], st_in.pop((hh, ch)))
        if i >= lookahead:
            hh, ch = items[i - lookahead]
            o = o1_ref[rows[ch], lanes[hh]] + inter.pop((hh, ch))
            o = o * lax.rsqrt(jnp.mean(o * o, axis=-1, keepdims=True) + NORM_EPS)
            o_ref[rows[ch], lanes[hh]] = (o * hg_ref[:, lanes[hh]]
                                          * _silu(g_ref[rows[ch], lanes[hh]])).astype(BF16)


def _scan_call(mid, lb_f, lb_b, hg_gain, *, row0, batch, seq, heads, cols, tc, tc_bwd,
               heads_per_step_fwd, heads_per_step_bwd):
    hd = HEAD_DIM
    nb = seq // tc
    rb0 = row0 // tc
    cq, czf, czb, cv, cg = (c // hd for c in cols)
    assert heads % heads_per_step_bwd == 0 and cg % heads_per_step_bwd == 0
    hpf = heads_per_step_fwd
    assert heads % hpf == 0 and all(col % hpf == 0 for col in (cq, czf, czb, cv))
    rows_per_tot = SCAN_CHUNK // SUBLANES
    hw = heads * hd
    wf = hd * hpf

    def tok(col):
        return pl.BlockSpec((tc, wf), lambda b, h, j: (rb0 + b * nb + j, col // hpf + h))

    def own(rows):
        return pl.BlockSpec((rows, wf), lambda b, h, j: (b * nb + j, h))

    par = pl.BlockSpec((1, wf), lambda b, h, j: (0, h))
    o1, qb, kb, tot, vt = pl.pallas_call(
        functools.partial(_scan_fwd_kernel, chunk=SCAN_CHUNK),
        grid=(batch, heads // hpf, nb),
        in_specs=[tok(cq), tok(czf), tok(czb), tok(cv), par, par],
        out_specs=[own(tc), own(tc), own(tc), own(tc // rows_per_tot),
                   pl.BlockSpec((wf, tc), lambda b, h, j: (h, b * nb + j))],
        out_shape=[jax.ShapeDtypeStruct((batch * seq, hw), F32),
                   jax.ShapeDtypeStruct((batch * seq, hw), BF16),
                   jax.ShapeDtypeStruct((batch * seq, hw), BF16),
                   jax.ShapeDtypeStruct((batch * seq // rows_per_tot, hw), F32),
                   jax.ShapeDtypeStruct((hw, batch * seq), BF16)],
        scratch_shapes=[pltpu.VMEM((hpf, hd, hd), F32)],
        compiler_params=_cparams(("parallel", "parallel", "arbitrary")),
        name="hgrn2_fwd",
    )(mid, mid, mid, mid, lb_f, lb_b)

    nb2 = seq // tc_bwd
    rb2 = row0 // tc_bwd

    hps = heads_per_step_bwd
    wd = hd * hps

    def own_rev(rows):
        return pl.BlockSpec((rows, wd), lambda b, h, j: (b * nb2 + nb2 - 1 - j, h))

    return pl.pallas_call(
        functools.partial(_scan_bwd_kernel, chunk=SCAN_CHUNK, lookahead=2),
        grid=(batch, heads // hps, nb2),
        in_specs=[own_rev(tc_bwd), own_rev(tc_bwd), own_rev(tc_bwd // rows_per_tot),
                  pl.BlockSpec((wd, tc_bwd), lambda b, h, j: (h, b * nb2 + nb2 - 1 - j)),
                  pl.BlockSpec((tc_bwd, wd),
                               lambda b, h, j: (rb2 + b * nb2 + nb2 - 1 - j, cg // hps + h)),
                  own_rev(tc_bwd), pl.BlockSpec((1, wd), lambda b, h, j: (0, h))],
        out_specs=own_rev(tc_bwd),
        out_shape=jax.ShapeDtypeStruct((batch * seq, hw), BF16),
        scratch_shapes=[pltpu.VMEM((hps, hd, hd), F32)],
        compiler_params=_cparams(("parallel", "parallel", "arbitrary")),
        name="hgrn2_bwd",
    )(qb, kb, tot, vt, mid, o1, hg_gain)


def _trunk(x, groups, params, *, ffn_tm, ffn_tf, proj_tm, proj_tn, out_tm, scan_tc, scan_tc_bwd):
    (ffn1_norm, ffn1_w_in, ffn1_w_down, mix_norm, w_in, lb_logits, hg_norm, w_fourier_out,
     w_hgrn_out, w_out, ffn2_norm, ffn2_w_in, ffn2_w_down, final_norm) = params
    depth, d = mix_norm.shape
    fw = w_fourier_out.shape[1]
    hw = w_hgrn_out.shape[1]
    heads = hw // HEAD_DIM
    d_ff = ffn1_w_down.shape[1]
    cols = tuple(j * hw for j in (0, 1, 2, 3, 4))

    def ffn_half(x, norm, w_in_l, w_down_l):
        return _ffn_call(x, norm[None, :], w_in_l[:, :d_ff].astype(BF16),
                         w_in_l[:, d_ff:].astype(BF16), w_down_l.astype(BF16),
                         tm=ffn_tm, tf=ffn_tf)

    lb = _lower_bound_call(lb_logits)
    dft = {seq: _dft_constants(seq, FOURIER_GROUP_DIM) for (_, _, seq) in groups}

    for l in range(depth):
        x = ffn_half(x, ffn1_norm[l], ffn1_w_in[l], ffn1_w_down[l])

        mid, gates = _proj_call(x, mix_norm[l][None, :], w_in[l].astype(BF16), tm=proj_tm,
                                tn=proj_tn, n_fourier=fw, n_gates=2 * d)
        fm_parts, ob_parts = [], []
        for (row0, batch, seq) in groups:
            fm_parts.append(_fourier_call(mid, dft[seq], row0=row0, batch=batch, seq=seq,
                                          col0=5 * hw, width=fw))
            ob_parts.append(_scan_call(mid, lb[0, l][None, :], lb[1, l][None, :],
                                       hg_norm[l][None, :], row0=row0, batch=batch, seq=seq,
                                       heads=heads, cols=cols, tc=scan_tc,
                                       tc_bwd=min(scan_tc_bwd, seq), heads_per_step_fwd=2,
                                       heads_per_step_bwd=4))
        x = _mixout_call(x, fm_parts, ob_parts, gates, w_fourier_out[l].astype(BF16),
                         w_hgrn_out[l].astype(BF16), w_out[l].astype(BF16), tm=out_tm)

        x = ffn_half(x, ffn2_norm[l], ffn2_w_in[l], ffn2_w_down[l])

    return _norm_call(x, final_norm[None, :], [batch * seq for (_, batch, seq) in groups],
                      tm=out_tm)


def kernel(x_prompt, x_sample, ffn1_norm, ffn1_w_in, ffn1_w_down, mix_norm, w_in, lb_logits, hg_norm, w_fourier_out, w_hgrn_out, w_out, ffn2_norm, ffn2_w_in, ffn2_w_down, final_norm):
    bp, sp, d = x_prompt.shape
    bs, ss, _ = x_sample.shape
    x = jnp.concatenate([x_prompt.reshape(bp * sp, d), x_sample.reshape(bs * ss, d)], axis=0)
    groups = [(0, bp, sp), (bp * sp, bs, ss)]
    params = (ffn1_norm, ffn1_w_in, ffn1_w_down, mix_norm, w_in, lb_logits, hg_norm,
              w_fourier_out, w_hgrn_out, w_out, ffn2_norm, ffn2_w_in, ffn2_w_down, final_norm)
    y_prompt, y_sample = _trunk(x, groups, params, ffn_tm=1024, ffn_tf=512, proj_tm=1024,
                                proj_tn=1024, out_tm=256, scan_tc=1024, scan_tc_bwd=2048)
    return (y_prompt.reshape(bp, sp, d), y_sample.reshape(bs, ss, d))
```

```python
import functools

import ml_dtypes
import numpy as np
import jax
import jax.numpy as jnp
from jax import lax
from jax.experimental import pallas as pl
from jax.experimental.pallas import tpu as pltpu

F32 = jnp.float32
BF16 = jnp.bfloat16

NORM_EPS = 1e-6
MIN_FORGET = 1e-30
LANES = 128
SUBLANES = 8
HEAD_DIM = 128
FOURIER_GROUP_DIM = 128
SCAN_CHUNK = 128
DFT_BLOCK = 128
VMEM_LIMIT = 56 * 1024 * 1024


def _cparams(sem):
    return pltpu.CompilerParams(dimension_semantics=sem, vmem_limit_bytes=VMEM_LIMIT)


def _rms_scale(x, gain):
    ms = jnp.mean(x * x, axis=-1, keepdims=True)
    return x * lax.rsqrt(ms + NORM_EPS) * gain


def _ffn_kernel(x_ref, gain_ref, wg_ref, wu_ref, wd_ref, o_ref, h_ref, *, last_cols, norm_rows):
    j = pl.program_id(1)
    last = pl.num_programs(1) - 1
    tf = wg_ref.shape[1]

    def act(h, cols):
        g = jnp.dot(h, wg_ref[:, :cols], preferred_element_type=F32)
        u = jnp.dot(h, wu_ref[:, :cols], preferred_element_type=F32)
        a = (g * jax.nn.sigmoid(g) * u * 0.5).astype(BF16)
        return jnp.dot(a, wd_ref[:cols, :], preferred_element_type=F32)

    @pl.when(j == 0)
    def _():
        chunks = [pl.ds(r, norm_rows) for r in range(0, x_ref.shape[0], norm_rows)]

        def normalize(rows):
            h = _rms_scale(x_ref[rows, :], gain_ref[...]).astype(BF16)
            h_ref[rows, :] = h
            return h

        h_next = normalize(chunks[0])
        for r, rows in enumerate(chunks):
            h = h_next
            if r + 1 < len(chunks):
                h_next = normalize(chunks[r + 1])
            o_ref[rows, :] = x_ref[rows, :] + act(h, tf)

    def step(cols):
        o_ref[...] += act(h_ref[...], cols)

    pl.when((j > 0) & (j < last))(lambda: step(tf))
    pl.when(j == last)(lambda: step(last_cols))


def _ffn_call(x, gain, w_gate, w_up, w_down, *, tm, tf):
    t, d = x.shape
    f = w_down.shape[0]
    nf = pl.cdiv(f, tf)
    last_cols = f - (nf - 1) * tf
    assert last_cols % LANES == 0 and nf > 1
    return pl.pallas_call(
        functools.partial(_ffn_kernel, last_cols=last_cols, norm_rows=min(tm, 256)),
        grid=(t // tm, nf),
        in_specs=[
            pl.BlockSpec((tm, d), lambda i, j: (i, 0)),
            pl.BlockSpec((1, d), lambda i, j: (0, 0)),
            pl.BlockSpec((d, tf), lambda i, j: (0, j)),
            pl.BlockSpec((d, tf), lambda i, j: (0, j)),
            pl.BlockSpec((tf, d), lambda i, j: (j, 0)),
        ],
        out_specs=pl.BlockSpec((tm, d), lambda i, j: (i, 0)),
        out_shape=jax.ShapeDtypeStruct((t, d), F32),
        scratch_shapes=[pltpu.VMEM((tm, d), BF16)],
        compiler_params=_cparams(("parallel", "arbitrary")),
        name="ffn_half",
    )(x, gain, w_gate, w_up, w_down)


def _proj_kernel(x_ref, gain_ref, wlo_ref, whi_ref, act_ref, gate_ref, h_ref, *, j_gate,
                 norm_rows):
    j = pl.program_id(1)
    half = wlo_ref.shape[1]

    def tile(o_ref, h, rows):
        o_ref[rows, :half] = jnp.dot(h, wlo_ref[...], preferred_element_type=F32)
        o_ref[rows, half:] = jnp.dot(h, whi_ref[...], preferred_element_type=F32)

    @pl.when(j == 0)
    def _():
        chunks = [pl.ds(r, norm_rows) for r in range(0, x_ref.shape[0], norm_rows)]

        def normalize(rows):
            h = _rms_scale(x_ref[rows, :], gain_ref[...]).astype(BF16)
            h_ref[rows, :] = h
            return h

        h_next = normalize(chunks[0])
        for r, rows in enumerate(chunks):
            h = h_next
            if r + 1 < len(chunks):
                h_next = normalize(chunks[r + 1])
            tile(act_ref, h, rows)

    everything = slice(None)
    pl.when((j > 0) & (j < j_gate))(lambda: tile(act_ref, h_ref[...], everything))
    pl.when(j >= j_gate)(lambda: tile(gate_ref, h_ref[...], everything))


def _proj_call(x, gain, w, *, tm, tn, n_fourier, n_gates):
    t, d = x.shape
    n = w.shape[1]
    half = tn // 2
    n_act = n - n_gates
    j_gate, n_steps = n_act // tn, n // tn
    nfb, nmb = n_fourier // half, (n_act - n_fourier) // half
    assert n_act % tn == 0 and n_gates % tn == 0 and n_fourier % half == 0

    def source_block(a):
        return jnp.where(a < nmb, a + nfb, jnp.where(a < nmb + nfb, a - nmb, a))

    return pl.pallas_call(
        functools.partial(_proj_kernel, j_gate=j_gate, norm_rows=min(tm, 256)),
        grid=(t // tm, n_steps),
        in_specs=[
            pl.BlockSpec((tm, d), lambda i, j: (i, 0)),
            pl.BlockSpec((1, d), lambda i, j: (0, 0)),
            pl.BlockSpec((d, half), lambda i, j: (0, source_block(2 * j))),
            pl.BlockSpec((d, half), lambda i, j: (0, source_block(2 * j + 1))),
        ],
        out_specs=[
            pl.BlockSpec((tm, tn), lambda i, j: (i, jnp.minimum(j, j_gate - 1))),
            pl.BlockSpec((tm, tn), lambda i, j: (i, jnp.maximum(j - j_gate, 0))),
        ],
        out_shape=[jax.ShapeDtypeStruct((t, n_act), F32),
                   jax.ShapeDtypeStruct((t, n_gates), F32)],
        scratch_shapes=[pltpu.VMEM((tm, d), BF16)],
        compiler_params=_cparams(("parallel", "arbitrary")),
        name="mixer_proj",
    )(x, gain, w, w)


def _sigmoid(x):
    return 0.5 + 0.5 * jnp.tanh(0.5 * x)


def _mixout_kernel(*refs, tile_starts):
    ng = len(tile_starts)
    x_ref = refs[0]
    fm_refs = refs[1:1 + ng]
    ob_refs = refs[1 + ng:1 + 2 * ng]
    ga_ref, gb_ref, wfo_ref, who_ref, wo_ref, o_ref = refs[1 + 2 * ng:]
    i = pl.program_id(0)
    fm, ob = fm_refs[0][...], ob_refs[0][...]
    for g in range(1, ng):
        in_g = i >= tile_starts[g]
        fm = jnp.where(in_g, fm_refs[g][...], fm)
        ob = jnp.where(in_g, ob_refs[g][...], ob)
    a = jnp.dot(fm.astype(BF16), wfo_ref[...], preferred_element_type=F32)
    b = jnp.dot(ob, who_ref[...], preferred_element_type=F32)
    m = (_sigmoid(ga_ref[...]) * a + _sigmoid(gb_ref[...]) * b).astype(BF16)
    o_ref[...] = x_ref[...] + jnp.dot(m, wo_ref[...], preferred_element_type=F32)


def _mixout_call(x, fm_parts, ob_parts, gates, w_fo, w_ho, w_out, *, tm):
    t, d = x.shape
    tile_starts, tile_counts, start = [], [], 0
    for part in fm_parts:
        tile_starts.append(start)
        tile_counts.append(part.shape[0] // tm)
        start += part.shape[0] // tm

    def group_spec(part, g):
        return pl.BlockSpec(
            (tm, part.shape[1]),
            lambda i: (jnp.clip(i - tile_starts[g], 0, tile_counts[g] - 1), 0))

    def resident(w):
        return pl.BlockSpec(w.shape, lambda i: (0, 0), pipeline_mode=pl.Buffered(1))

    return pl.pallas_call(
        functools.partial(_mixout_kernel, tile_starts=tuple(tile_starts)),
        grid=(t // tm,),
        in_specs=(
            [pl.BlockSpec((tm, d), lambda i: (i, 0))]
            + [group_spec(p, g) for g, p in enumerate(fm_parts)]
            + [group_spec(p, g) for g, p in enumerate(ob_parts)]
            + [pl.BlockSpec((tm, d), lambda i: (i, 0)), pl.BlockSpec((tm, d), lambda i: (i, 1)),
               resident(w_fo), resident(w_ho), resident(w_out)]),
        out_specs=pl.BlockSpec((tm, d), lambda i: (i, 0)),
        out_shape=jax.ShapeDtypeStruct((t, d), F32),
        compiler_params=_cparams(("parallel",)),
        name="mixer_out",
    )(x, *fm_parts, *ob_parts, gates, gates, w_fo, w_ho, w_out)


def _norm_kernel(x_ref, gain_ref, *o_refs, tile_starts):
    i = pl.program_id(0)
    bounds = tuple(tile_starts) + (pl.num_programs(0),)
    for g, o_ref in enumerate(o_refs):
        @pl.when((i >= bounds[g]) & (i < bounds[g + 1]))
        def _():
            o_ref[...] = _rms_scale(x_ref[...], gain_ref[...])


def _norm_call(x, gain, group_rows, *, tm):
    t, d = x.shape
    tile_starts, tile_counts, start = [], [], 0
    for rows in group_rows:
        tile_starts.append(start)
        tile_counts.append(rows // tm)
        start += rows // tm

    def group_spec(g):
        return pl.BlockSpec(
            (tm, d), lambda i: (jnp.clip(i - tile_starts[g], 0, tile_counts[g] - 1), 0))

    return pl.pallas_call(
        functools.partial(_norm_kernel, tile_starts=tuple(tile_starts)),
        grid=(t // tm,),
        in_specs=[pl.BlockSpec((tm, d), lambda i: (i, 0)), pl.BlockSpec((1, d), lambda i: (0, 0))],
        out_specs=[group_spec(g) for g in range(len(group_rows))],
        out_shape=[jax.ShapeDtypeStruct((rows, d), F32) for rows in group_rows],
        compiler_params=_cparams(("arbitrary",)),
        name="final_norm",
    )(x, gain)


def _lower_bound_kernel(logit_ref, o_ref, *, depth):
    for d0 in range(0, logit_ref.shape[0], depth):
        rows = [logit_ref[d0 + l:d0 + l + 1, :] for l in range(depth)]
        mx = functools.reduce(jnp.maximum, rows)
        es = [jnp.exp(r - mx) for r in rows]
        tot = functools.reduce(lambda a, b: a + b, es)
        ps = [e / tot for e in es]
        acc = ps[0]
        for l in range(depth):
            if l:
                acc = acc + ps[l]
            o_ref[d0 + l:d0 + l + 1, :] = jnp.maximum(acc - ps[0], 0.0)


def _lower_bound_call(lb_logits):
    dirs, depth, hw = lb_logits.shape
    lb = pl.pallas_call(
        functools.partial(_lower_bound_kernel, depth=depth),
        out_shape=jax.ShapeDtypeStruct((dirs * depth, hw), F32),
        name="lower_bounds",
    )(lb_logits.astype(F32).reshape(dirs * depth, hw))
    return lb.reshape(dirs, depth, hw)


def _split2_np(m):
    hi = m.astype(ml_dtypes.bfloat16).astype(np.float64)
    lo = m - hi
    return hi, lo


def _split2(x):
    hi = x.astype(BF16)
    lo = (x - hi.astype(F32)).astype(BF16)
    return hi, lo


def _dft_constants(s, gd):
    nr = DFT_BLOCK
    kr = np.arange(nr, dtype=np.float64)
    cr = np.cos(2 * np.pi * np.outer(kr, kr) / nr)
    sr = np.sin(2 * np.pi * np.outer(kr, kr) / nr)
    mr = np.block([[cr, sr], [-sr, cr]])
    mrh, mrl = _split2_np(mr)
    mrcat = np.concatenate([mrh, mrh, mrl], axis=1)

    kc = np.arange(gd, dtype=np.float64)
    scale = 1.0 / np.sqrt(float(s) * gd)
    cc = np.cos(2 * np.pi * np.outer(kc, kc) / gd) * scale
    sc = np.sin(2 * np.pi * np.outer(kc, kc) / gd) * scale
    mc = np.concatenate([cc, sc], axis=0)
    mch, mcl = _split2_np(mc)
    mccat = np.concatenate([mch, mch, mcl], axis=0)

    k2 = np.arange(s // nr, dtype=np.float64)
    ang = (2 * np.pi * np.outer(k2, kr) / s).reshape(s, 1)
    twc = jnp.broadcast_to(jnp.asarray(np.cos(ang), F32), (s, LANES))
    tws = jnp.broadcast_to(jnp.asarray(np.sin(ang), F32), (s, LANES))
    return jnp.asarray(mrcat, BF16), jnp.asarray(mccat, BF16), twc, tws


def _c_add(a, b):
    return b if a is None else a if b is None else a + b


def _c_sub(a, b):
    return a if b is None else -b if a is None else a - b


def _fft_blocks(xs):
    n = len(xs)
    if n == 1:
        return xs
    even, odd = _fft_blocks(xs[0::2]), _fft_blocks(xs[1::2])
    out = [None] * n
    for k in range(n // 2):
        o_r, o_i = odd[k]
        if k == 0:
            t_r, t_i = o_r, o_i
        elif 4 * k == n:
            t_r, t_i = o_i, (None if o_r is None else -o_r)
        else:
            wr, wi = float(np.cos(2 * np.pi * k / n)), float(np.sin(2 * np.pi * k / n))
            t_r = _c_add(None if o_r is None else wr * o_r, None if o_i is None else wi * o_i)
            t_i = _c_sub(None if o_i is None else wr * o_i, None if o_r is None else wi * o_r)
        e_r, e_i = even[k]
        out[k] = (_c_add(e_r, t_r), _c_add(e_i, t_i))
        out[k + n // 2] = (_c_sub(e_r, t_r), _c_sub(e_i, t_i))
    return out


def _dft_kernel(x_ref, twc_ref, tws_ref, mr_ref, mc_ref, o_ref):
    nr = DFT_BLOCK
    nb = x_ref.shape[0] // nr

    def rows(ref, j):
        return ref[pl.ds(j * nr, nr), :]

    gd = x_ref.shape[1]
    u = _fft_blocks([(rows(x_ref, j), None) for j in range(nb)])

    def twiddled(k2):
        u_r, u_i = u[k2]
        u_r = jnp.zeros((nr, gd), F32) if u_r is None else u_r
        u_i = jnp.zeros_like(u_r) if u_i is None else u_i
        if k2 == 0:
            return jnp.concatenate([u_r, u_i], axis=0)
        c, s = rows(twc_ref, k2), rows(tws_ref, k2)
        return jnp.concatenate([u_r * c + u_i * s, u_i * c - u_r * s], axis=0)

    group = min(nb, 2 * LANES // gd)
    for k0 in range(0, nb, group):
        th, tl = _split2(jnp.concatenate([twiddled(k0 + g) for g in range(group)], axis=1))
        v = jnp.dot(mr_ref[...], jnp.concatenate([th, tl, th], axis=0),
                    preferred_element_type=F32)
        for g in range(group):
            sl = slice(g * gd, (g + 1) * gd)
            vh, vl = _split2(jnp.concatenate([v[:nr, sl], v[nr:, sl]], axis=1))
            o_ref[0, :, k0 + g, :] = jnp.dot(jnp.concatenate([vh, vl, vh], axis=1), mc_ref[...],
                                             preferred_element_type=F32)


def _fourier_call(act, consts, *, row0, batch, seq, col0, width):
    gd = FOURIER_GROUP_DIM
    nb = seq // DFT_BLOCK
    mrcat, mccat, twc, tws = consts

    def resident(a):
        return pl.BlockSpec(a.shape, lambda b, g: (0, 0), pipeline_mode=pl.Buffered(1))

    y = pl.pallas_call(
        _dft_kernel,
        grid=(batch, width // gd),
        in_specs=[pl.BlockSpec((seq, gd), lambda b, g: (row0 // seq + b, col0 // gd + g)),
                  resident(twc), resident(tws), resident(mrcat), resident(mccat)],
        out_specs=pl.BlockSpec((1, DFT_BLOCK, nb, gd), lambda b, g: (b, 0, 0, g)),
        out_shape=jax.ShapeDtypeStruct((batch, DFT_BLOCK, nb, width), F32),
        compiler_params=_cparams(("parallel", "parallel")),
        name="dft2d",
    )(act, twc, tws, mrcat, mccat)
    return y.reshape(batch * seq, width)


def _silu(x):
    h = 0.5 * x
    return h + h * jnp.tanh(h)


def _gate_consts(lb):
    c1 = 0.5 * (1.0 - lb)
    return lb + c1, c1


def _gates(z, c0, c1):
    f = c0 + c1 * jnp.tanh(0.5 * z)
    f_floor = jnp.maximum(f, MIN_FORGET)
    return f_floor, jnp.log2(f_floor), 1.0 - f


def _cumsum_rows(x, tri):
    w = x.shape[1]
    hi = x.astype(BF16)
    lo = (x - hi.astype(F32)).astype(BF16)
    y = jnp.dot(tri, jnp.concatenate([hi, lo], axis=1), preferred_element_type=F32)
    return y[:, :w] + y[:, w:]


def _boundary_rows(x, n, row8):
    c, w = x.shape
    parts = []
    if 2 * n >= SUBLANES:
        for p in range(c // (2 * n)):
            r = p * 2 * n + n - 1
            parts.append(jnp.broadcast_to(x[r:r + 1, :], (2 * n, w)))
    else:
        lo = row8 < 4
        for p in range(c // SUBLANES):
            r = p * SUBLANES
            a = jnp.broadcast_to(x[r + 1:r + 2, :], (SUBLANES, w))
            b = jnp.broadcast_to(x[r + 5:r + 6, :], (SUBLANES, w))
            parts.append(jnp.where(lo, a, b))
    return jnp.concatenate(parts, axis=0)


def _nt_dot(a, b):
    return lax.dot_general(a, b, (((1,), (1,)), ((), ())), preferred_element_type=F32)


def _intra_scores(q, kf, kb, ff, fb, b_f, bx_b, b_b, row, row8, xor):
    c = q.shape[0]
    products = []
    n = c // 2
    while n >= 1:
        if n >= SUBLANES:
            qt_parts, kt_parts = [], []
            for lo in range(0, c, 2 * n):
                mid, hi = lo + n, lo + 2 * n
                bf_m, bb_m = b_f[mid - 1:mid, :], b_b[mid - 1:mid, :]
                qt_parts.append(q[lo:mid] * jnp.exp2(bb_m - bx_b[lo:mid]))
                kt_parts.append(kf[lo:mid] * jnp.exp2(bf_m - b_f[lo:mid]))
                qt_parts.append(q[mid:hi] * jnp.exp2(b_f[mid:hi] - bf_m))
                kt_parts.append(kb[mid:hi] * jnp.exp2(bx_b[mid:hi] - bb_m))
            qt, kt = jnp.concatenate(qt_parts, axis=0), jnp.concatenate(kt_parts, axis=0)
        else:
            odd = (row & n) != 0
            kk = jnp.where(odd, kb, kf)
            if n == 1:
                qt = q * jnp.where(odd, ff, fb)
                kt = kk
            else:
                d1 = b_f - _boundary_rows(b_f, n, row8)
                d2 = _boundary_rows(b_b, n, row8) - bx_b
                qt = q * jnp.exp2(jnp.minimum(d1, d2))
                kt = kk * jnp.exp2(-jnp.maximum(d1, d2))
        products.append((n, _nt_dot(qt.astype(BF16), kt.astype(BF16))))
        n //= 2
    return products


def _assemble_scores(products, xor):
    p = None
    for n, pn in products:
        p = pn if p is None else jnp.where(xor < 2 * n, pn, p)
    return jnp.where(xor == 0, 0.0, p)


def _scan_fwd_kernel(q_ref, zf_ref, zb_ref, v_ref, lbf_ref, lbb_ref,
                     o_ref, qb_ref, kb_ref, tot_ref, vt_ref, st_ref, *, chunk):
    @pl.when(pl.program_id(2) == 0)
    def _():
        st_ref[...] = jnp.zeros_like(st_ref)

    c = chunk
    w = HEAD_DIM
    n_heads = st_ref.shape[0]
    ri = lax.broadcasted_iota(jnp.int32, (c, c), 0)
    ci = lax.broadcasted_iota(jnp.int32, (c, c), 1)
    tri = (ri >= ci).astype(BF16)
    xor = ri ^ ci
    row = lax.broadcasted_iota(jnp.int32, (c, w), 0)
    row8 = lax.broadcasted_iota(jnp.int32, (SUBLANES, w), 0)
    lanes = [pl.ds(hh * w, w) for hh in range(n_heads)]
    consts = [_gate_consts(lbf_ref[:, ln]) + _gate_consts(lbb_ref[:, ln]) for ln in lanes]

    nch = q_ref.shape[0] // c
    work = [dict(sl=pl.ds(ch * c, c), ch=ch, hh=hh) for hh in range(n_heads) for ch in range(nch)]

    def stage_gates(s):
        sl, ln = s["sl"], lanes[s["hh"]]
        c0f, c1f, c0b, c1b = consts[s["hh"]]
        s["q"] = _silu(q_ref[sl, ln])
        s["v"] = v_ref[sl, ln]
        s["ff"], lf, s["kf"] = _gates(zf_ref[sl, ln], c0f, c1f)
        s["fb"], s["lfb"], s["kb"] = _gates(zb_ref[sl, ln], c0b, c1b)
        s["b_f"] = _cumsum_rows(lf, tri)
        s["b_b"] = _cumsum_rows(s["lfb"], tri)

    def stage_levels(s):
        sl, ln = s["sl"], lanes[s["hh"]]
        q, v, kf, kb, b_f, b_b = s["q"], s["v"], s["kf"], s["kb"], s["b_f"], s["b_b"]
        bx_b = b_b - s["lfb"]
        s["products"] = _intra_scores(q, kf, kb, s["ff"], s["fb"], b_f, bx_b, b_b, row, row8, xor)
        s["diag"] = jnp.sum(q * (kf + kb), axis=-1, keepdims=True)
        b_last = b_f[c - 1:c, :]
        s["qh"] = (q * jnp.exp2(b_f)).astype(BF16)
        kh = (kf * jnp.exp2(b_last - b_f)).astype(BF16)
        vt = v.T.astype(BF16)
        vt_ref[ln, sl] = vt
        s["upd"] = jnp.dot(vt, kh, preferred_element_type=F32)
        s["decay"] = jnp.exp2(b_last)
        tot = b_b[c - 1:c, :]
        qb_ref[sl, ln] = (q * jnp.exp2(tot - bx_b)).astype(BF16)
        kb_ref[sl, ln] = (kb * jnp.exp2(bx_b)).astype(BF16)
        tot_ref[pl.ds(s["ch"] * SUBLANES, SUBLANES), ln] = jnp.broadcast_to(jnp.exp2(tot),
                                                                             (SUBLANES, w))

    def stage_scores(s):
        p = _assemble_scores(s.pop("products"), xor)
        s["pv"] = jnp.dot(p.astype(BF16), s["v"].astype(BF16), preferred_element_type=F32)

    stages = (stage_gates, stage_levels, stage_scores)
    for step in range(len(work) + len(stages) - 1):
        for k, stage in enumerate(stages):
            if 0 <= step - k < len(work):
                stage(work[step - k])

    for hh in range(n_heads):
        st = st_ref[hh]
        for s in work[hh * nch:(hh + 1) * nch]:
            o_ref[s["sl"], lanes[hh]] = (s["pv"] + s["diag"] * s["v"]
                                         + _nt_dot(s["qh"], st.astype(BF16)))
            st = st * s["decay"] + s["upd"]
        st_ref[hh] = st


def _scan_bwd_kernel(qb_ref, kb_ref, tot_ref, vt_ref, g_ref, o1_ref, hg_ref, o_ref, st_ref, *,
                     chunk, lookahead):
    @pl.when(pl.program_id(2) == 0)
    def _():
        st_ref[...] = jnp.zeros_like(st_ref)

    c = chunk
    hd = HEAD_DIM
    nch = o1_ref.shape[0] // c
    items = [(hh, ch) for hh in range(st_ref.shape[0]) for ch in reversed(range(nch))]
    rows = {ch: pl.ds(ch * c, c) for ch in range(nch)}
    lanes = {hh: pl.ds(hh * hd, hd) for hh in range(st_ref.shape[0])}

    upd = {(hh, ch): jnp.dot(vt_ref[lanes[hh], rows[ch]], kb_ref[rows[ch], lanes[hh]],
                             preferred_element_type=F32) for hh, ch in items}
    st_in = {}
    for hh in range(st_ref.shape[0]):
        st = st_ref[hh]
        for ch in reversed(range(nch)):
            st_in[hh, ch] = st.astype(BF16)
            st = st * tot_ref[pl.ds(ch * SUBLANES, 1), lanes[hh]] + upd[hh, ch]
        st_ref[hh] = st

    inter = {}
    for i in range(len(items) + lookahead):
        if i < len(items):
            hh, ch = items[i]
            inter[hh, ch] = _nt_dot(qb_ref[rows[ch], lanes[hh]], st_in.pop((hh, ch)))
        if i >= lookahead:
            hh, ch = items[i - lookahead]
            o = o1_ref[rows[ch], lanes[hh]] + inter.pop((hh, ch))
            o = o * lax.rsqrt(jnp.mean(o * o, axis=-1, keepdims=True) + NORM_EPS)
            o_ref[rows[ch], lanes[hh]] = (o * hg_ref[:, lanes[hh]]
                                          * _silu(g_ref[rows[ch], lanes[hh]])).astype(BF16)


def _scan_call(mid, lb_f, lb_b, hg_gain, *, row0, batch, seq, heads, cols, tc, tc_bwd,
               heads_per_step_fwd, heads_per_step_bwd):
    hd = HEAD_DIM
    nb = seq // tc
    rb0 = row0 // tc
    cq, czf, czb, cv, cg = (c // hd for c in cols)
    assert heads % heads_per_step_bwd == 0 and cg % heads_per_step_bwd == 0
    hpf = heads_per_step_fwd
    assert heads % hpf == 0 and all(col % hpf == 0 for col in (cq, czf, czb, cv))
    rows_per_tot = SCAN_CHUNK // SUBLANES
    hw = heads * hd
    wf = hd * hpf

    def tok(col):
        return pl.BlockSpec((tc, wf), lambda b, h, j: (rb0 + b * nb + j, col // hpf + h))

    def own(rows):
        return pl.BlockSpec((rows, wf), lambda b, h, j: (b * nb + j, h))

    par = pl.BlockSpec((1, wf), lambda b, h, j: (0, h))
    o1, qb, kb, tot, vt = pl.pallas_call(
        functools.partial(_scan_fwd_kernel, chunk=SCAN_CHUNK),
        grid=(batch, heads // hpf, nb),
        in_specs=[tok(cq), tok(czf), tok(czb), tok(cv), par, par],
        out_specs=[own(tc), own(tc), own(tc), own(tc // rows_per_tot),
                   pl.BlockSpec((wf, tc), lambda b, h, j: (h, b * nb + j))],
        out_shape=[jax.ShapeDtypeStruct((batch * seq, hw), F32),
                   jax.ShapeDtypeStruct((batch * seq, hw), BF16),
                   jax.ShapeDtypeStruct((batch * seq, hw), BF16),
                   jax.ShapeDtypeStruct((batch * seq // rows_per_tot, hw), F32),
                   jax.ShapeDtypeStruct((hw, batch * seq), BF16)],
        scratch_shapes=[pltpu.VMEM((hpf, hd, hd), F32)],
        compiler_params=_cparams(("parallel", "parallel", "arbitrary")),
        name="hgrn2_fwd",
    )(mid, mid, mid, mid, lb_f, lb_b)

    nb2 = seq // tc_bwd
    rb2 = row0 // tc_bwd

    hps = heads_per_step_bwd
    wd = hd * hps

    def own_rev(rows):
        return pl.BlockSpec((rows, wd), lambda b, h, j: (b * nb2 + nb2 - 1 - j, h))

    return pl.pallas_call(
        functools.partial(_scan_bwd_kernel, chunk=SCAN_CHUNK, lookahead=2),
        grid=(batch, heads // hps, nb2),
        in_specs=[own_rev(tc_bwd), own_rev(tc_bwd), own_rev(tc_bwd // rows_per_tot),
                  pl.BlockSpec((wd, tc_bwd), lambda b, h, j: (h, b * nb2 + nb2 - 1 - j)),
                  pl.BlockSpec((tc_bwd, wd),
                               lambda b, h, j: (rb2 + b * nb2 + nb2 - 1 - j, cg // hps + h)),
                  own_rev(tc_bwd), pl.BlockSpec((1, wd), lambda b, h, j: (0, h))],
        out_specs=own_rev(tc_bwd),
        out_shape=jax.ShapeDtypeStruct((batch * seq, hw), BF16),
        scratch_shapes=[pltpu.VMEM((hps, hd, hd), F32)],
        compiler_params=_cparams(("parallel", "parallel", "arbitrary")),
        name="hgrn2_bwd",
    )(qb, kb, tot, vt, mid, o1, hg_gain)


def _trunk(x, groups, params, *, ffn_tm, ffn_tf, proj_tm, proj_tn, out_tm, scan_tc, scan_tc_bwd):
    (ffn1_norm, ffn1_w_in, ffn1_w_down, mix_norm, w_in, lb_logits, hg_norm, w_fourier_out,
     w_hgrn_out, w_out, ffn2_norm, ffn2_w_in, ffn2_w_down, final_norm) = params
    depth, d = mix_norm.shape
    fw = w_fourier_out.shape[1]
    hw = w_hgrn_out.shape[1]
    heads = hw // HEAD_DIM
    d_ff = ffn1_w_down.shape[1]
    cols = tuple(j * hw for j in (0, 1, 2, 3, 4))

    def ffn_half(x, norm, w_in_l, w_down_l):
        return _ffn_call(x, norm[None, :], w_in_l[:, :d_ff].astype(BF16),
                         w_in_l[:, d_ff:].astype(BF16), w_down_l.astype(BF16),
                         tm=ffn_tm, tf=ffn_tf)

    lb = _lower_bound_call(lb_logits)
    dft = {seq: _dft_constants(seq, FOURIER_GROUP_DIM) for (_, _, seq) in groups}

    for l in range(depth):
        x = ffn_half(x, ffn1_norm[l], ffn1_w_in[l], ffn1_w_down[l])

        mid, gates = _proj_call(x, mix_norm[l][None, :], w_in[l].astype(BF16), tm=proj_tm,
                                tn=proj_tn, n_fourier=fw, n_gates=2 * d)
        fm_parts, ob_parts = [], []
        for (row0, batch, seq) in groups:
            fm_parts.append(_fourier_call(mid, dft[seq], row0=row0, batch=batch, seq=seq,
                                          col0=5 * hw, width=fw))
            ob_parts.append(_scan_call(mid, lb[0, l][None, :], lb[1, l][None, :],
                                       hg_norm[l][None, :], row0=row0, batch=batch, seq=seq,
                                       heads=heads, cols=cols, tc=scan_tc,
                                       tc_bwd=min(scan_tc_bwd, seq), heads_per_step_fwd=2,
                                       heads_per_step_bwd=4))
        x = _mixout_call(x, fm_parts, ob_parts, gates, w_fourier_out[l].astype(BF16),
                         w_hgrn_out[l].astype(BF16), w_out[l].astype(BF16), tm=out_tm)

        x = ffn_half(x, ffn2_norm[l], ffn2_w_in[l], ffn2_w_down[l])

    return _norm_call(x, final_norm[None, :], [batch * seq for (_, batch, seq) in groups],
                      tm=out_tm)


def kernel(x_prompt, x_sample, ffn1_norm, ffn1_w_in, ffn1_w_down, mix_norm, w_in, lb_logits, hg_norm, w_fourier_out, w_hgrn_out, w_out, ffn2_norm, ffn2_w_in, ffn2_w_down, final_norm):
    bp, sp, d = x_prompt.shape
    bs, ss, _ = x_sample.shape
    x = jnp.concatenate([x_prompt.reshape(bp * sp, d), x_sample.reshape(bs * ss, d)], axis=0)
    groups = [(0, bp, sp), (bp * sp, bs, ss)]
    params = (ffn1_norm, ffn1_w_in, ffn1_w_down, mix_norm, w_in, lb_logits, hg_norm,
              w_fourier_out, w_hgrn_out, w_out, ffn2_norm, ffn2_w_in, ffn2_w_down, final_norm)
    y_prompt, y_sample = _trunk(x, groups, params, ffn_tm=1024, ffn_tf=512, proj_tm=1024,
                                proj_tn=1024, out_tm=256, scan_tc=1024, scan_tc_bwd=2048)
    return (y_prompt.reshape(bp, sp, d), y_sample.reshape(bs, ss, d))
```

```python
import functools

import ml_dtypes
import numpy as np
import jax
import jax.numpy as jnp
from jax import lax
from jax.experimental import pallas as pl
from jax.experimental.pallas import tpu as pltpu

F32 = jnp.float32
BF16 = jnp.bfloat16

NORM_EPS = 1e-6
MIN_FORGET = 1e-30
LANES = 128
SUBLANES = 8
HEAD_DIM = 128
FOURIER_GROUP_DIM = 128
SCAN_CHUNK = 128
DFT_BLOCK = 128
VMEM_LIMIT = 56 * 1024 * 1024


def _cparams(sem):
    return pltpu.CompilerParams(dimension_semantics=sem, vmem_limit_bytes=VMEM_LIMIT)


def _rms_scale(x, gain):
    ms = jnp.mean(x * x, axis=-1, keepdims=True)
    return x * lax.rsqrt(ms + NORM_EPS) * gain


def _ffn_kernel(x_ref, gain_ref, wg_ref, wu_ref, wd_ref, o_ref, h_ref, *, last_cols, norm_rows):
    j = pl.program_id(1)
    last = pl.num_programs(1) - 1
    tf = wg_ref.shape[1]

    def act(h, cols):
        g = jnp.dot(h, wg_ref[:, :cols], preferred_element_type=F32)
        u = jnp.dot(h, wu_ref[:, :cols], preferred_element_type=F32)
        a = (g * jax.nn.sigmoid(g) * u * 0.5).astype(BF16)
        return jnp.dot(a, wd_ref[:cols, :], preferred_element_type=F32)

    @pl.when(j == 0)
    def _():
        chunks = [pl.ds(r, norm_rows) for r in range(0, x_ref.shape[0], norm_rows)]

        def normalize(rows):
            h = _rms_scale(x_ref[rows, :], gain_ref[...]).astype(BF16)
            h_ref[rows, :] = h
            return h

        h_next = normalize(chunks[0])
        for r, rows in enumerate(chunks):
            h = h_next
            if r + 1 < len(chunks):
                h_next = normalize(chunks[r + 1])
            o_ref[rows, :] = x_ref[rows, :] + act(h, tf)

    def step(cols):
        o_ref[...] += act(h_ref[...], cols)

    pl.when((j > 0) & (j < last))(lambda: step(tf))
    pl.when(j == last)(lambda: step(last_cols))


def _ffn_call(x, gain, w_gate, w_up, w_down, *, tm, tf):
    t, d = x.shape
    f = w_down.shape[0]
    nf = pl.cdiv(f, tf)
    last_cols = f - (nf - 1) * tf
    assert last_cols % LANES == 0 and nf > 1
    return pl.pallas_call(
        functools.partial(_ffn_kernel, last_cols=last_cols, norm_rows=min(tm, 256)),
        grid=(t // tm, nf),
        in_specs=[
            pl.BlockSpec((tm, d), lambda i, j: (i, 0)),
            pl.BlockSpec((1, d), lambda i, j: (0, 0)),
            pl.BlockSpec((d, tf), lambda i, j: (0, j)),
            pl.BlockSpec((d, tf), lambda i, j: (0, j)),
            pl.BlockSpec((tf, d), lambda i, j: (j, 0)),
        ],
        out_specs=pl.BlockSpec((tm, d), lambda i, j: (i, 0)),
        out_shape=jax.ShapeDtypeStruct((t, d), F32),
        scratch_shapes=[pltpu.VMEM((tm, d), BF16)],
        compiler_params=_cparams(("parallel", "arbitrary")),
        name="ffn_half",
    )(x, gain, w_gate, w_up, w_down)


def _proj_kernel(x_ref, gain_ref, wlo_ref, whi_ref, act_ref, gate_ref, h_ref, *, j_gate,
                 norm_rows):
    j = pl.program_id(1)
    half = wlo_ref.shape[1]

    def tile(o_ref, h, rows):
        o_ref[rows, :half] = jnp.dot(h, wlo_ref[...], preferred_element_type=F32)
        o_ref[rows, half:] = jnp.dot(h, whi_ref[...], preferred_element_type=F32)

    @pl.when(j == 0)
    def _():
        chunks = [pl.ds(r, norm_rows) for r in range(0, x_ref.shape[0], norm_rows)]

        def normalize(rows):
            h = _rms_scale(x_ref[rows, :], gain_ref[...]).astype(BF16)
            h_ref[rows, :] = h
            return h

        h_next = normalize(chunks[0])
        for r, rows in enumerate(chunks):
            h = h_next
            if r + 1 < len(chunks):
                h_next = normalize(chunks[r + 1])
            tile(act_ref, h, rows)

    everything = slice(None)
    pl.when((j > 0) & (j < j_gate))(lambda: tile(act_ref, h_ref[...], everything))
    pl.when(j >= j_gate)(lambda: tile(gate_ref, h_ref[...], everything))


def _proj_call(x, gain, w, *, tm, tn, n_fourier, n_gates):
    t, d = x.shape
    n = w.shape[1]
    half = tn // 2
    n_act = n - n_gates
    j_gate, n_steps = n_act // tn, n // tn
    nfb, nmb = n_fourier // half, (n_act - n_fourier) // half
    assert n_act % tn == 0 and n_gates % tn == 0 and n_fourier % half == 0

    def source_block(a):
        return jnp.where(a < nmb, a + nfb, jnp.where(a < nmb + nfb, a - nmb, a))

    return pl.pallas_call(
        functools.partial(_proj_kernel, j_gate=j_gate, norm_rows=min(tm, 256)),
        grid=(t // tm, n_steps),
        in_specs=[
            pl.BlockSpec((tm, d), lambda i, j: (i, 0)),
            pl.BlockSpec((1, d), lambda i, j: (0, 0)),
            pl.BlockSpec((d, half), lambda i, j: (0, source_block(2 * j))),
            pl.BlockSpec((d, half), lambda i, j: (0, source_block(2 * j + 1))),
        ],
        out_specs=[
            pl.BlockSpec((tm, tn), lambda i, j: (i, jnp.minimum(j, j_gate - 1))),
            pl.BlockSpec((tm, tn), lambda i, j: (i, jnp.maximum(j - j_gate, 0))),
        ],
        out_shape=[jax.ShapeDtypeStruct((t, n_act), F32),
                   jax.ShapeDtypeStruct((t, n_gates), F32)],
        scratch_shapes=[pltpu.VMEM((tm, d), BF16)],
        compiler_params=_cparams(("parallel", "arbitrary")),
        name="mixer_proj",
    )(x, gain, w, w)


def _sigmoid(x):
    return 0.5 + 0.5 * jnp.tanh(0.5 * x)


def _mixout_kernel(*refs, tile_starts):
    ng = len(tile_starts)
    x_ref = refs[0]
    fm_refs = refs[1:1 + ng]
    ob_refs = refs[1 + ng:1 + 2 * ng]
    ga_ref, gb_ref, wfo_ref, who_ref, wo_ref, o_ref = refs[1 + 2 * ng:]
    i = pl.program_id(0)
    fm, ob = fm_refs[0][...], ob_refs[0][...]
    for g in range(1, ng):
        in_g = i >= tile_starts[g]
        fm = jnp.where(in_g, fm_refs[g][...], fm)
        ob = jnp.where(in_g, ob_refs[g][...], ob)
    a = jnp.dot(fm.astype(BF16), wfo_ref[...], preferred_element_type=F32)
    b = jnp.dot(ob, who_ref[...], preferred_element_type=F32)
    m = (_sigmoid(ga_ref[...]) * a + _sigmoid(gb_ref[...]) * b).astype(BF16)
    o_ref[...] = x_ref[...] + jnp.dot(m, wo_ref[...], preferred_element_type=F32)


def _mixout_call(x, fm_parts, ob_parts, gates, w_fo, w_ho, w_out, *, tm):
    t, d = x.shape
    tile_starts, tile_counts, start = [], [], 0
    for part in fm_parts:
        tile_starts.append(start)
        tile_counts.append(part.shape[0] // tm)
        start += part.shape[0] // tm

    def group_spec(part, g):
        return pl.BlockSpec(
            (tm, part.shape[1]),
            lambda i: (jnp.clip(i - tile_starts[g], 0, tile_counts[g] - 1), 0))

    def resident(w):
        return pl.BlockSpec(w.shape, lambda i: (0, 0), pipeline_mode=pl.Buffered(1))

    return pl.pallas_call(
        functools.partial(_mixout_kernel, tile_starts=tuple(tile_starts)),
        grid=(t // tm,),
        in_specs=(
            [pl.BlockSpec((tm, d), lambda i: (i, 0))]
            + [group_spec(p, g) for g, p in enumerate(fm_parts)]
            + [group_spec(p, g) for g, p in enumerate(ob_parts)]
            + [pl.BlockSpec((tm, d), lambda i: (i, 0)), pl.BlockSpec((tm, d), lambda i: (i, 1)),
               resident(w_fo), resident(w_ho), resident(w_out)]),
        out_specs=pl.BlockSpec((tm, d), lambda i: (i, 0)),
        out_shape=jax.ShapeDtypeStruct((t, d), F32),
        compiler_params=_cparams(("parallel",)),
        name="mixer_out",
    )(x, *fm_parts, *ob_parts, gates, gates, w_fo, w_ho, w_out)


def _norm_kernel(x_ref, gain_ref, *o_refs, tile_starts):
    i = pl.program_id(0)
    bounds = tuple(tile_starts) + (pl.num_programs(0),)
    for g, o_ref in enumerate(o_refs):
        @pl.when((i >= bounds[g]) & (i < bounds[g + 1]))
        def _():
            o_ref[...] = _rms_scale(x_ref[...], gain_ref[...])


def _norm_call(x, gain, group_rows, *, tm):
    t, d = x.shape
    tile_starts, tile_counts, start = [], [], 0
    for rows in group_rows:
        tile_starts.append(start)
        tile_counts.append(rows // tm)
        start += rows // tm

    def group_spec(g):
        return pl.BlockSpec(
            (tm, d), lambda i: (jnp.clip(i - tile_starts[g], 0, tile_counts[g] - 1), 0))

    return pl.pallas_call(
        functools.partial(_norm_kernel, tile_starts=tuple(tile_starts)),
        grid=(t // tm,),
        in_specs=[pl.BlockSpec((tm, d), lambda i: (i, 0)), pl.BlockSpec((1, d), lambda i: (0, 0))],
        out_specs=[group_spec(g) for g in range(len(group_rows))],
        out_shape=[jax.ShapeDtypeStruct((rows, d), F32) for rows in group_rows],
        compiler_params=_cparams(("arbitrary",)),
        name="final_norm",
    )(x, gain)


def _lower_bound_kernel(logit_ref, o_ref, *, depth):
    for d0 in range(0, logit_ref.shape[0], depth):
        rows = [logit_ref[d0 + l:d0 + l + 1, :] for l in range(depth)]
        mx = functools.reduce(jnp.maximum, rows)
        es = [jnp.exp(r - mx) for r in rows]
        tot = functools.reduce(lambda a, b: a + b, es)
        ps = [e / tot for e in es]
        acc = ps[0]
        for l in range(depth):
            if l:
                acc = acc + ps[l]
            o_ref[d0 + l:d0 + l + 1, :] = jnp.maximum(acc - ps[0], 0.0)


def _lower_bound_call(lb_logits):
    dirs, depth, hw = lb_logits.shape
    lb = pl.pallas_call(
        functools.partial(_lower_bound_kernel, depth=depth),
        out_shape=jax.ShapeDtypeStruct((dirs * depth, hw), F32),
        name="lower_bounds",
    )(lb_logits.astype(F32).reshape(dirs * depth, hw))
    return lb.reshape(dirs, depth, hw)


def _split2_np(m):
    hi = m.astype(ml_dtypes.bfloat16).astype(np.float64)
    lo = m - hi
    return hi, lo


def _split2(x):
    hi = x.astype(BF16)
    lo = (x - hi.astype(F32)).astype(BF16)
    return hi, lo


def _dft_constants(s, gd):
    nr = DFT_BLOCK
    kr = np.arange(nr, dtype=np.float64)
    cr = np.cos(2 * np.pi * np.outer(kr, kr) / nr)
    sr = np.sin(2 * np.pi * np.outer(kr, kr) / nr)
    mr = np.block([[cr, sr], [-sr, cr]])
    mrh, mrl = _split2_np(mr)
    mrcat = np.concatenate([mrh, mrh, mrl], axis=1)

    kc = np.arange(gd, dtype=np.float64)
    scale = 1.0 / np.sqrt(float(s) * gd)
    cc = np.cos(2 * np.pi * np.outer(kc, kc) / gd) * scale
    sc = np.sin(2 * np.pi * np.outer(kc, kc) / gd) * scale
    mc = np.concatenate([cc, sc], axis=0)
    mch, mcl = _split2_np(mc)
    mccat = np.concatenate([mch, mch, mcl], axis=0)

    k2 = np.arange(s // nr, dtype=np.float64)
    ang = (2 * np.pi * np.outer(k2, kr) / s).reshape(s, 1)
    twc = jnp.broadcast_to(jnp.asarray(np.cos(ang), F32), (s, LANES))
    tws = jnp.broadcast_to(jnp.asarray(np.sin(ang), F32), (s, LANES))
    return jnp.asarray(mrcat, BF16), jnp.asarray(mccat, BF16), twc, tws


def _c_add(a, b):
    return b if a is None else a if b is None else a + b


def _c_sub(a, b):
    return a if b is None else -b if a is None else a - b


def _fft_blocks(xs):
    n = len(xs)
    if n == 1:
        return xs
    even, odd = _fft_blocks(xs[0::2]), _fft_blocks(xs[1::2])
    out = [None] * n
    for k in range(n // 2):
        o_r, o_i = odd[k]
        if k == 0:
            t_r, t_i = o_r, o_i
        elif 4 * k == n:
            t_r, t_i = o_i, (None if o_r is None else -o_r)
        else:
            wr, wi = float(np.cos(2 * np.pi * k / n)), float(np.sin(2 * np.pi * k / n))
            t_r = _c_add(None if o_r is None else wr * o_r, None if o_i is None else wi * o_i)
            t_i = _c_sub(None if o_i is None else wr * o_i, None if o_r is None else wi * o_r)
        e_r, e_i = even[k]
        out[k] = (_c_add(e_r, t_r), _c_add(e_i, t_i))
        out[k + n // 2] = (_c_sub(e_r, t_r), _c_sub(e_i, t_i))
    return out


def _dft_kernel(x_ref, twc_ref, tws_ref, mr_ref, mc_ref, o_ref):
    nr = DFT_BLOCK
    nb = x_ref.shape[0] // nr

    def rows(ref, j):
        return ref[pl.ds(j * nr, nr), :]

    gd = x_ref.shape[1]
    u = _fft_blocks([(rows(x_ref, j), None) for j in range(nb)])

    def twiddled(k2):
        u_r, u_i = u[k2]
        u_r = jnp.zeros((nr, gd), F32) if u_r is None else u_r
        u_i = jnp.zeros_like(u_r) if u_i is None else u_i
        if k2 == 0:
            return jnp.concatenate([u_r, u_i], axis=0)
        c, s = rows(twc_ref, k2), rows(tws_ref, k2)
        return jnp.concatenate([u_r * c + u_i * s, u_i * c - u_r * s], axis=0)

    group = min(nb, 2 * LANES // gd)
    for k0 in range(0, nb, group):
        th, tl = _split2(jnp.concatenate([twiddled(k0 + g) for g in range(group)], axis=1))
        v = jnp.dot(mr_ref[...], jnp.concatenate([th, tl, th], axis=0),
                    preferred_element_type=F32)
        for g in range(group):
            sl = slice(g * gd, (g + 1) * gd)
            vh, vl = _split2(jnp.concatenate([v[:nr, sl], v[nr:, sl]], axis=1))
            o_ref[0, :, k0 + g, :] = jnp.dot(jnp.concatenate([vh, vl, vh], axis=1), mc_ref[...],
                                             preferred_element_type=F32)


def _fourier_call(act, consts, *, row0, batch, seq, col0, width):
    gd = FOURIER_GROUP_DIM
    nb = seq // DFT_BLOCK
    mrcat, mccat, twc, tws = consts

    def resident(a):
        return pl.BlockSpec(a.shape, lambda b, g: (0, 0), pipeline_mode=pl.Buffered(1))

    y = pl.pallas_call(
        _dft_kernel,
        grid=(batch, width // gd),
        in_specs=[pl.BlockSpec((seq, gd), lambda b, g: (row0 // seq + b, col0 // gd + g)),
                  resident(twc), resident(tws), resident(mrcat), resident(mccat)],
        out_specs=pl.BlockSpec((1, DFT_BLOCK, nb, gd), lambda b, g: (b, 0, 0, g)),
        out_shape=jax.ShapeDtypeStruct((batch, DFT_BLOCK, nb, width), F32),
        compiler_params=_cparams(("parallel", "parallel")),
        name="dft2d",
    )(act, twc, tws, mrcat, mccat)
    return y.reshape(batch * seq, width)


def _silu(x):
    h = 0.5 * x
    return h + h * jnp.tanh(h)


def _gate_consts(lb):
    c1 = 0.5 * (1.0 - lb)
    return lb + c1, c1


def _gates(z, c0, c1):
    f = c0 + c1 * jnp.tanh(0.5 * z)
    f_floor = jnp.maximum(f, MIN_FORGET)
    return f_floor, jnp.log2(f_floor), 1.0 - f


def _cumsum_rows(x, tri):
    w = x.shape[1]
    hi = x.astype(BF16)
    lo = (x - hi.astype(F32)).astype(BF16)
    y = jnp.dot(tri, jnp.concatenate([hi, lo], axis=1), preferred_element_type=F32)
    return y[:, :w] + y[:, w:]


def _boundary_rows(x, n, row8):
    c, w = x.shape
    parts = []
    if 2 * n >= SUBLANES:
        for p in range(c // (2 * n)):
            r = p * 2 * n + n - 1
            parts.append(jnp.broadcast_to(x[r:r + 1, :], (2 * n, w)))
    else:
        lo = row8 < 4
        for p in range(c // SUBLANES):
            r = p * SUBLANES
            a = jnp.broadcast_to(x[r + 1:r + 2, :], (SUBLANES, w))
            b = jnp.broadcast_to(x[r + 5:r + 6, :], (SUBLANES, w))
            parts.append(jnp.where(lo, a, b))
    return jnp.concatenate(parts, axis=0)


def _nt_dot(a, b):
    return lax.dot_general(a, b, (((1,), (1,)), ((), ())), preferred_element_type=F32)


def _intra_scores(q, kf, kb, ff, fb, b_f, bx_b, b_b, row, row8, xor):
    c = q.shape[0]
    products = []
    n = c // 2
    while n >= 1:
        if n >= SUBLANES:
            qt_parts, kt_parts = [], []
            for lo in range(0, c, 2 * n):
                mid, hi = lo + n, lo + 2 * n
                bf_m, bb_m = b_f[mid - 1:mid, :], b_b[mid - 1:mid, :]
                qt_parts.append(q[lo:mid] * jnp.exp2(bb_m - bx_b[lo:mid]))
                kt_parts.append(kf[lo:mid] * jnp.exp2(bf_m - b_f[lo:mid]))
                qt_parts.append(q[mid:hi] * jnp.exp2(b_f[mid:hi] - bf_m))
                kt_parts.append(kb[mid:hi] * jnp.exp2(bx_b[mid:hi] - bb_m))
            qt, kt = jnp.concatenate(qt_parts, axis=0), jnp.concatenate(kt_parts, axis=0)
        else:
            odd = (row & n) != 0
            kk = jnp.where(odd, kb, kf)
            if n == 1:
                qt = q * jnp.where(odd, ff, fb)
                kt = kk
            else:
                d1 = b_f - _boundary_rows(b_f, n, row8)
                d2 = _boundary_rows(b_b, n, row8) - bx_b
                qt = q * jnp.exp2(jnp.minimum(d1, d2))
                kt = kk * jnp.exp2(-jnp.maximum(d1, d2))
        products.append((n, _nt_dot(qt.astype(BF16), kt.astype(BF16))))
        n //= 2
    return products


def _assemble_scores(products, xor):
    p = None
    for n, pn in products:
        p = pn if p is None else jnp.where(xor < 2 * n, pn, p)
    return jnp.where(xor == 0, 0.0, p)


def _scan_fwd_kernel(q_ref, zf_ref, zb_ref, v_ref, lbf_ref, lbb_ref,
                     o_ref, qb_ref, kb_ref, tot_ref, vt_ref, st_ref, *, chunk):
    @pl.when(pl.program_id(2) == 0)
    def _():
        st_ref[...] = jnp.zeros_like(st_ref)

    c = chunk
    w = HEAD_DIM
    n_heads = st_ref.shape[0]
    ri = lax.broadcasted_iota(jnp.int32, (c, c), 0)
    ci = lax.broadcasted_iota(jnp.int32, (c, c), 1)
    tri = (ri >= ci).astype(BF16)
    xor = ri ^ ci
    row = lax.broadcasted_iota(jnp.int32, (c, w), 0)
    row8 = lax.broadcasted_iota(jnp.int32, (SUBLANES, w), 0)
    lanes = [pl.ds(hh * w, w) for hh in range(n_heads)]
    consts = [_gate_consts(lbf_ref[:, ln]) + _gate_consts(lbb_ref[:, ln]) for ln in lanes]

    nch = q_ref.shape[0] // c
    work = [dict(sl=pl.ds(ch * c, c), ch=ch, hh=hh) for hh in range(n_heads) for ch in range(nch)]

    def stage_gates(s):
        sl, ln = s["sl"], lanes[s["hh"]]
        c0f, c1f, c0b, c1b = consts[s["hh"]]
        s["q"] = _silu(q_ref[sl, ln])
        s["v"] = v_ref[sl, ln]
        s["ff"], lf, s["kf"] = _gates(zf_ref[sl, ln], c0f, c1f)
        s["fb"], s["lfb"], s["kb"] = _gates(zb_ref[sl, ln], c0b, c1b)
        s["b_f"] = _cumsum_rows(lf, tri)
        s["b_b"] = _cumsum_rows(s["lfb"], tri)

    def stage_levels(s):
        sl, ln = s["sl"], lanes[s["hh"]]
        q, v, kf, kb, b_f, b_b = s["q"], s["v"], s["kf"], s["kb"], s["b_f"], s["b_b"]
        bx_b = b_b - s["lfb"]
        s["products"] = _intra_scores(q, kf, kb, s["ff"], s["fb"], b_f, bx_b, b_b, row, row8, xor)
        s["diag"] = jnp.sum(q * (kf + kb), axis=-1, keepdims=True)
        b_last = b_f[c - 1:c, :]
        s["qh"] = (q * jnp.exp2(b_f)).astype(BF16)
        kh = (kf * jnp.exp2(b_last - b_f)).astype(BF16)
        vt = v.T.astype(BF16)
        vt_ref[ln, sl] = vt
        s["upd"] = jnp.dot(vt, kh, preferred_element_type=F32)
        s["decay"] = jnp.exp2(b_last)
        tot = b_b[c - 1:c, :]
        qb_ref[sl, ln] = (q * jnp.exp2(tot - bx_b)).astype(BF16)
        kb_ref[sl, ln] = (kb * jnp.exp2(bx_b)).astype(BF16)
        tot_ref[pl.ds(s["ch"] * SUBLANES, SUBLANES), ln] = jnp.broadcast_to(jnp.exp2(tot),
                                                                             (SUBLANES, w))

    def stage_scores(s):
        p = _assemble_scores(s.pop("products"), xor)
        s["pv"] = jnp.dot(p.astype(BF16), s["v"].astype(BF16), preferred_element_type=F32)

    stages = (stage_gates, stage_levels, stage_scores)
    for step in range(len(work) + len(stages) - 1):
        for k, stage in enumerate(stages):
            if 0 <= step - k < len(work):
                stage(work[step - k])

    for hh in range(n_heads):
        st = st_ref[hh]
        for s in work[hh * nch:(hh + 1) * nch]:
            o_ref[s["sl"], lanes[hh]] = (s["pv"] + s["diag"] * s["v"]
                                         + _nt_dot(s["qh"], st.astype(BF16)))
            st = st * s["decay"] + s["upd"]
        st_ref[hh] = st


def _scan_bwd_kernel(qb_ref, kb_ref, tot_ref, vt_ref, g_ref, o1_ref, hg_ref, o_ref, st_ref, *,
                     chunk, lookahead):
    @pl.when(pl.program_id(2) == 0)
    def _():
        st_ref[...] = jnp.zeros_like(st_ref)

    c = chunk
    hd = HEAD_DIM
    nch = o1_ref.shape[0] // c
    items = [(hh, ch) for hh in range(st_ref.shape[0]) for ch in reversed(range(nch))]
    rows = {ch: pl.ds(ch * c, c) for ch in range(nch)}
    lanes = {hh: pl.ds(hh * hd, hd) for hh in range(st_ref.shape[0])}

    upd = {(hh, ch): jnp.dot(vt_ref[lanes[hh], rows[ch]], kb_ref[rows[ch], lanes[hh]],
                             preferred_element_type=F32) for hh, ch in items}
    st_in = {}
    for hh in range(st_ref.shape[0]):
        st = st_ref[hh]
        for ch in reversed(range(nch)):
            st_in[hh, ch] = st.astype(BF16)
            st = st * tot_ref[pl.ds(ch * SUBLANES, 1), lanes[hh]] + upd[hh, ch]
        st_ref[hh] = st

    inter = {}
    for i in range(len(items) + lookahead):
        if i < len(items):
            hh, ch = items[i]
            inter[hh, ch] = _nt_dot(qb_ref[rows[ch], lanes[hh]], st_in.pop((hh, ch)))
        if i >= lookahead:
            hh, ch = items[i - lookahead]
            o = o1_ref[rows[ch], lanes[hh]] + inter.pop((hh, ch))
            o = o * lax.rsqrt(jnp.mean(o * o, axis=-1, keepdims=True) + NORM_EPS)
            o_ref[rows[ch], lanes[hh]] = (o * hg_ref[:, lanes[hh]]
                                          * _silu(g_ref[rows[ch], lanes[hh]])).astype(BF16)


def _scan_call(mid, lb_f, lb_b, hg_gain, *, row0, batch, seq, heads, cols, tc, tc_bwd,
               heads_per_step_fwd, heads_per_step_bwd):
    hd = HEAD_DIM
    nb = seq // tc
    rb0 = row0 // tc
    cq, czf, czb, cv, cg = (c // hd for c in cols)
    assert heads % heads_per_step_bwd == 0 and cg % heads_per_step_bwd == 0
    hpf = heads_per_step_fwd
    assert heads % hpf == 0 and all(col % hpf == 0 for col in (cq, czf, czb, cv))
    rows_per_tot = SCAN_CHUNK // SUBLANES
    hw = heads * hd
    wf = hd * hpf

    def tok(col):
        return pl.BlockSpec((tc, wf), lambda b, h, j: (rb0 + b * nb + j, col // hpf + h))

    def own(rows):
        return pl.BlockSpec((rows, wf), lambda b, h, j: (b * nb + j, h))

    par = pl.BlockSpec((1, wf), lambda b, h, j: (0, h))
    o1, qb, kb, tot, vt = pl.pallas_call(
        functools.partial(_scan_fwd_kernel, chunk=SCAN_CHUNK),
        grid=(batch, heads // hpf, nb),
        in_specs=[tok(cq), tok(czf), tok(czb), tok(cv), par, par],
        out_specs=[own(tc), own(tc), own(tc), own(tc // rows_per_tot),
                   pl.BlockSpec((wf, tc), lambda b, h, j: (h, b * nb + j))],
        out_shape=[jax.ShapeDtypeStruct((batch * seq, hw), F32),
                   jax.ShapeDtypeStruct((batch * seq, hw), BF16),
                   jax.ShapeDtypeStruct((batch * seq, hw), BF16),
                   jax.ShapeDtypeStruct((batch * seq // rows_per_tot, hw), F32),
                   jax.ShapeDtypeStruct((hw, batch * seq), BF16)],
        scratch_shapes=[pltpu.VMEM((hpf, hd, hd), F32)],
        compiler_params=_cparams(("parallel", "parallel", "arbitrary")),
        name="hgrn2_fwd",
    )(mid, mid, mid, mid, lb_f, lb_b)

    nb2 = seq // tc_bwd
    rb2 = row0 // tc_bwd

    hps = heads_per_step_bwd
    wd = hd * hps

    def own_rev(rows):
        return pl.BlockSpec((rows, wd), lambda b, h, j: (b * nb2 + nb2 - 1 - j, h))

    return pl.pallas_call(
        functools.partial(_scan_bwd_kernel, chunk=SCAN_CHUNK, lookahead=2),
        grid=(batch, heads // hps, nb2),
        in_specs=[own_rev(tc_bwd), own_rev(tc_bwd), own_rev(tc_bwd // rows_per_tot),
                  pl.BlockSpec((wd, tc_bwd), lambda b, h, j: (h, b * nb2 + nb2 - 1 - j)),
                  pl.BlockSpec((tc_bwd, wd),
                               lambda b, h, j: (rb2 + b * nb2 + nb2 - 1 - j, cg // hps + h)),
                  own_rev(tc_bwd), pl.BlockSpec((1, wd), lambda b, h, j: (0, h))],
        out_specs=own_rev(tc_bwd),
        out_shape=jax.ShapeDtypeStruct((batch * seq, hw), BF16),
        scratch_shapes=[pltpu.VMEM((hps, hd, hd), F32)],
        compiler_params=_cparams(("parallel", "parallel", "arbitrary")),
        name="hgrn2_bwd",
    )(qb, kb, tot, vt, mid, o1, hg_gain)


def _trunk(x, groups, params, *, ffn_tm, ffn_tf, proj_tm, proj_tn, out_tm, scan_tc, scan_tc_bwd):
    (ffn1_norm, ffn1_w_in, ffn1_w_down, mix_norm, w_in, lb_logits, hg_norm, w_fourier_out,
     w_hgrn_out, w_out, ffn2_norm, ffn2_w_in, ffn2_w_down, final_norm) = params
    depth, d = mix_norm.shape
    fw = w_fourier_out.shape[1]
    hw = w_hgrn_out.shape[1]
    heads = hw // HEAD_DIM
    d_ff = ffn1_w_down.shape[1]
    cols = tuple(j * hw for j in (0, 1, 2, 3, 4))

    def ffn_half(x, norm, w_in_l, w_down_l):
        return _ffn_call(x, norm[None, :], w_in_l[:, :d_ff].astype(BF16),
                         w_in_l[:, d_ff:].astype(BF16), w_down_l.astype(BF16),
                         tm=ffn_tm, tf=ffn_tf)

    lb = _lower_bound_call(lb_logits)
    dft = {seq: _dft_constants(seq, FOURIER_GROUP_DIM) for (_, _, seq) in groups}

    for l in range(depth):
        x = ffn_half(x, ffn1_norm[l], ffn1_w_in[l], ffn1_w_down[l])

        mid, gates = _proj_call(x, mix_norm[l][None, :], w_in[l].astype(BF16), tm=proj_tm,
                                tn=proj_tn, n_fourier=fw, n_gates=2 * d)
        fm_parts, ob_parts = [], []
        for (row0, batch, seq) in groups:
            fm_parts.append(_fourier_call(mid, dft[seq], row0=row0, batch=batch, seq=seq,
                                          col0=5 * hw, width=fw))
            ob_parts.append(_scan_call(mid, lb[0, l][None, :], lb[1, l][None, :],
                                       hg_norm[l][None, :], row0=row0, batch=batch, seq=seq,
                                       heads=heads, cols=cols, tc=scan_tc,
                                       tc_bwd=min(scan_tc_bwd, seq), heads_per_step_fwd=4,
                                       heads_per_step_bwd=4))
        x = _mixout_call(x, fm_parts, ob_parts, gates, w_fourier_out[l].astype(BF16),
                         w_hgrn_out[l].astype(BF16), w_out[l].astype(BF16), tm=out_tm)

        x = ffn_half(x, ffn2_norm[l], ffn2_w_in[l], ffn2_w_down[l])

    return _norm_call(x, final_norm[None, :], [batch * seq for (_, batch, seq) in groups],
                      tm=out_tm)


def kernel(x_prompt, x_sample, ffn1_norm, ffn1_w_in, ffn1_w_down, mix_norm, w_in, lb_logits, hg_norm, w_fourier_out, w_hgrn_out, w_out, ffn2_norm, ffn2_w_in, ffn2_w_down, final_norm):
    bp, sp, d = x_prompt.shape
    bs, ss, _ = x_sample.shape
    x = jnp.concatenate([x_prompt.reshape(bp * sp, d), x_sample.reshape(bs * ss, d)], axis=0)
    groups = [(0, bp, sp), (bp * sp, bs, ss)]
    params = (ffn1_norm, ffn1_w_in, ffn1_w_down, mix_norm, w_in, lb_logits, hg_norm,
              w_fourier_out, w_hgrn_out, w_out, ffn2_norm, ffn2_w_in, ffn2_w_down, final_norm)
    y_prompt, y_sample = _trunk(x, groups, params, ffn_tm=1024, ffn_tf=512, proj_tm=1024,
                                proj_tn=1024, out_tm=256, scan_tc=1024, scan_tc_bwd=2048)
    return (y_prompt.reshape(bp, sp, d), y_sample.reshape(bs, ss, d))
```
